```python
import jax
import jax.numpy as jnp
from jax import lax
import numpy as np

D_MODEL = 1024
BATCH = 32
SEQ = 2048
DEPTH = 1
DEC_BATCH = 8
DEC_SEQ = 32
PAST_LEN = 4096

CHUNK = 64
N_META = 16
N_Q = 16
N_KV = 4
GROUP = N_Q // N_KV
HEAD_DIM = 64
ATTN_W = N_Q * HEAD_DIM
KV_W = N_KV * HEAD_DIM
WINDOW = 128
WIN_CHUNKS = WINDOW // CHUNK
D_CONV = D_MODEL
CONV_W = 3
N_EXPERTS = 32
TOP_K = 4
D_FF = D_MODEL
SWIGLU_LIMIT = 7.0
SWIGLU_ALPHA = 1.702
EXPERT_BLOCK = 256
LN_EPS = 1e-5
ALPHA = (2 * DEPTH) ** 0.25
BETA = (8 * DEPTH) ** -0.25
ATTN_SCALE = HEAD_DIM ** -0.5

OFF_GA = ATTN_W
OFF_GC = OFF_GA + D_MODEL
OFF_CB = OFF_GC + D_MODEL
OFF_K = OFF_CB + D_CONV
OFF_V = OFF_K + KV_W
OFF_CC = OFF_V + KV_W
OFF_CH = OFF_CC + D_CONV
IN_COLS = OFF_CH + D_CONV

kernel_name = 'hybrid_swa_shortconv_moe_stream_step'


def layer_norm(x, g, b):
    xf = x.astype(jnp.float32)
    mu = xf.mean(-1, keepdims=True)
    xc = xf - mu
    var = (xc * xc).mean(-1, keepdims=True)
    return (xc * lax.rsqrt(var + LN_EPS)).astype(x.dtype) * g + b


def kv_and_conv_input(z_tail):
    b, t = z_tail.shape[0], z_tail.shape[1]
    k = z_tail[..., :KV_W].reshape(b, t, N_KV, HEAD_DIM)
    v = z_tail[..., KV_W:2 * KV_W].reshape(b, t, N_KV, HEAD_DIM)
    u = z_tail[..., 2 * KV_W:2 * KV_W + D_CONV] * z_tail[..., 2 * KV_W + D_CONV:]
    return k, v, u


def banded_attention(q, k, v, past_k, past_v, past_valid, meta_k, meta_v, sinks):
    b, t = q.shape[0], q.shape[1]
    nc = -(-t // CHUNK)
    tp = nc * CHUNK
    padw = ((0, 0), (0, tp - t), (0, 0), (0, 0))
    qb = jnp.pad(q, padw).reshape(b, nc, CHUNK, N_KV, GROUP, HEAD_DIM)
    kb = jnp.concatenate([past_k.astype(k.dtype), jnp.pad(k, padw)], axis=1).reshape(b, nc + WIN_CHUNKS, CHUNK, N_KV, HEAD_DIM)
    vb = jnp.concatenate([past_v.astype(v.dtype), jnp.pad(v, padw)], axis=1).reshape(b, nc + WIN_CHUNKS, CHUNK, N_KV, HEAD_DIM)
    valid = jnp.concatenate([jnp.full((WINDOW,), past_valid), jnp.arange(tp) < t]).reshape(nc + WIN_CHUNKS, CHUNK)
    kw = jnp.concatenate([kb[:, j:j + nc] for j in range(WIN_CHUNKS + 1)], axis=2)
    vw = jnp.concatenate([vb[:, j:j + nc] for j in range(WIN_CHUNKS + 1)], axis=2)
    mw = jnp.concatenate([valid[j:j + nc] for j in range(WIN_CHUNKS + 1)], axis=1)
    s = jnp.einsum('bcqkgd,bcnkd->bckgqn', qb, kw).astype(jnp.float32) * ATTN_SCALE
    s = jnp.where(mw[None, :, None, None, None, :], s, -jnp.inf)
    sink = sinks.astype(jnp.float32).reshape(1, 1, N_KV, GROUP, 1, 1)
    m = jnp.maximum(s.max(axis=-1, keepdims=True), sink)
    if meta_k is not None:
        sm = jnp.einsum('bcqkgd,bmkd->bckgqm', qb, meta_k.astype(q.dtype)).astype(jnp.float32) * ATTN_SCALE
        m = jnp.maximum(m, sm.max(axis=-1, keepdims=True))
        pm = jnp.exp(sm - m)
    p = jnp.exp(s - m)
    den = p.sum(-1, keepdims=True) + jnp.exp(sink - m)
    if meta_k is not None:
        den = den + pm.sum(-1, keepdims=True)
    o = jnp.einsum('bckgqn,bcnkd->bcqkgd', (p / den).astype(v.dtype), vw)
    if meta_k is not None:
        o = o + jnp.einsum('bckgqm,bmkd->bcqkgd', (pm / den).astype(v.dtype), meta_v.astype(v.dtype))
    return o.reshape(b, tp, ATTN_W)[:, :t]


def causal_depthwise_conv(u, buf, w):
    t = u.shape[1]
    up = jnp.concatenate([buf.astype(u.dtype), u], axis=1)
    out = w[0] * up[:, 0:t]
    for j in range(1, CONV_W):
        out = out + w[j] * up[:, j:j + t]
    return out


def moe(h, w_router, b_router, w_gu, b_gu, w_d, b_d):
    b, t, d = h.shape
    n = b * t
    xt = h.reshape(n, d)
    logits = (xt @ w_router + b_router).astype(jnp.float32)
    top_val, top_idx = lax.top_k(logits, TOP_K)
    gate = jax.nn.softmax(top_val, axis=-1).astype(h.dtype)
    nk = n * TOP_K
    e_flat = top_idx.reshape(-1)
    order = jnp.argsort(e_flat)
    e_sorted = e_flat[order]
    tok_sorted = order // TOP_K
    counts = jnp.bincount(e_flat, length=N_EXPERTS)
    padded = (counts + EXPERT_BLOCK - 1) // EXPERT_BLOCK * EXPERT_BLOCK
    pend = jnp.cumsum(padded)
    poff = pend - padded
    start = jnp.cumsum(counts) - counts
    dest = poff[e_sorted] + jnp.arange(nk) - start[e_sorted]
    nb = -(-nk // EXPERT_BLOCK) + N_EXPERTS
    xbuf = jnp.zeros((nb * EXPERT_BLOCK, d), h.dtype).at[dest].set(xt[tok_sorted])
    block_e = jnp.minimum(jnp.searchsorted(pend, jnp.arange(nb) * EXPERT_BLOCK, side='right'), N_EXPERTS - 1)

    def expert_block(args):
        xb, e = args
        hgu = xb @ w_gu[e] + b_gu[e]
        g = jnp.minimum(hgu[:, 0::2], SWIGLU_LIMIT)
        up = jnp.clip(hgu[:, 1::2], -SWIGLU_LIMIT, SWIGLU_LIMIT)
        act = (up + 1.0) * (g * jax.nn.sigmoid(g * SWIGLU_ALPHA))
        return act @ w_d[e] + b_d[e]

    ybuf = lax.map(expert_block, (xbuf.reshape(nb, EXPERT_BLOCK, d), block_e)).reshape(nb * EXPERT_BLOCK, d)
    w_sorted = gate.reshape(-1)[order]
    y = jax.ops.segment_sum(ybuf[dest] * w_sorted[:, None], tok_sorted, num_segments=n)
    return y.reshape(b, t, d)


def layer_forward(h, meta_k, meta_v, past_k, past_v, past_valid, conv_buf,
                  w_in, b_in, conv_w, sinks, w_attn_br, w_conv_br, w_o, ln1_g, ln1_b,
                  w_router, b_router, w_gu, b_gu, w_d, b_d, ln2_g, ln2_b):
    b, t, _ = h.shape
    z = h @ w_in + b_in
    q = z[..., :OFF_GA].reshape(b, t, N_Q, HEAD_DIM)
    g_attn = jax.nn.sigmoid(z[..., OFF_GA:OFF_GC])
    g_conv = jax.nn.sigmoid(z[..., OFF_GC:OFF_CB])
    conv_b = z[..., OFF_CB:OFF_K]
    k, v, u = kv_and_conv_input(z[..., OFF_K:])
    o_attn = banded_attention(q, k, v, past_k, past_v, past_valid, meta_k, meta_v, sinks)
    o_conv = conv_b * causal_depthwise_conv(u, conv_buf, conv_w)
    mixed = (g_attn * (o_attn @ w_attn_br) + g_conv * (o_conv @ w_conv_br)) @ w_o
    h1 = layer_norm(ALPHA * h + mixed, ln1_g, ln1_b)
    h2 = layer_norm(ALPHA * h1 + moe(h1, w_router, b_router, w_gu, b_gu, w_d, b_d), ln2_g, ln2_b)
    return h2, k, v, u


def setup_inputs(seed: int = 0) -> dict:
    key = jax.random.key(seed)
    ks = jax.random.split(key, 28)
    f32 = jnp.float32

    def nrm(k, shape, s):
        return jax.random.normal(k, shape, f32) * s

    col_scale = jnp.ones((IN_COLS,), f32).at[OFF_V:OFF_CC].set(BETA)
    return {
        'x_prompt': nrm(ks[0], (BATCH, SEQ, D_MODEL), 1.0),
        'x_sample': nrm(ks[1], (DEC_BATCH, DEC_SEQ, D_MODEL), 1.0),
        'cache_meta_k': nrm(ks[2], (DEPTH, DEC_BATCH, N_META, N_KV, HEAD_DIM), 1.0),
        'cache_meta_v': nrm(ks[3], (DEPTH, DEC_BATCH, N_META, N_KV, HEAD_DIM), 1.0),
        'cache_win_k': nrm(ks[4], (DEPTH, DEC_BATCH, WINDOW, N_KV, HEAD_DIM), 1.0),
        'cache_win_v': nrm(ks[5], (DEPTH, DEC_BATCH, WINDOW, N_KV, HEAD_DIM), 1.0),
        'state_conv': nrm(ks[6], (DEPTH, DEC_BATCH, CONV_W - 1, D_CONV), 1.0),
        'meta_tokens': nrm(ks[7], (N_META, D_MODEL), 1.0),
        'ln_in_g': 1.0 + nrm(ks[8], (D_MODEL,), 0.02),
        'ln_in_b': nrm(ks[9], (D_MODEL,), 0.02),
        'w_in': nrm(ks[10], (DEPTH, D_MODEL, IN_COLS), D_MODEL ** -0.5) * col_scale,
        'b_in': nrm(ks[11], (DEPTH, IN_COLS), 0.02),
        'conv_w': nrm(ks[12], (DEPTH, CONV_W, D_CONV), CONV_W ** -0.5),
        'attn_sinks': nrm(ks[13], (DEPTH, N_Q), 0.5),
        'w_attn_br': nrm(ks[14], (DEPTH, ATTN_W, D_MODEL), BETA * ATTN_W ** -0.5),
        'w_conv_br': nrm(ks[15], (DEPTH, D_CONV, D_MODEL), BETA * D_CONV ** -0.5),
        'w_o': nrm(ks[16], (DEPTH, D_MODEL, D_MODEL), BETA * D_MODEL ** -0.5),
        'ln1_g': 1.0 + nrm(ks[17], (DEPTH, D_MODEL), 0.02),
        'ln1_b': nrm(ks[18], (DEPTH, D_MODEL), 0.02),
        'w_router': nrm(ks[19], (DEPTH, D_MODEL, N_EXPERTS), D_MODEL ** -0.5),
        'b_router': nrm(ks[20], (DEPTH, N_EXPERTS), 0.01),
        'w_gu': nrm(ks[21], (DEPTH, N_EXPERTS, D_MODEL, 2 * D_FF), BETA * D_MODEL ** -0.5),
        'b_gu': nrm(ks[22], (DEPTH, N_EXPERTS, 2 * D_FF), 0.02),
        'w_d': nrm(ks[23], (DEPTH, N_EXPERTS, D_FF, D_MODEL), BETA * D_FF ** -0.5),
        'b_d': nrm(ks[24], (DEPTH, N_EXPERTS, D_MODEL), 0.02),
        'ln2_g': 1.0 + nrm(ks[25], (DEPTH, D_MODEL), 0.02),
        'ln2_b': nrm(ks[26], (DEPTH, D_MODEL), 0.02),
    }


def reference(x_prompt, x_sample, cache_meta_k, cache_meta_v, cache_win_k, cache_win_v, state_conv,
              meta_tokens, ln_in_g, ln_in_b, w_in, b_in, conv_w, attn_sinks, w_attn_br, w_conv_br, w_o,
              ln1_g, ln1_b, w_router, b_router, w_gu, b_gu, w_d, b_d, ln2_g, ln2_b):
    b_p = x_prompt.shape[0]
    hp = layer_norm(x_prompt, ln_in_g, ln_in_b)
    hs = layer_norm(x_sample, ln_in_g, ln_in_b)
    hm = layer_norm(meta_tokens[None], ln_in_g, ln_in_b)
    mk_p, mv_p, wk_p, wv_p, cv_p, wk_s, wv_s, cv_s = [], [], [], [], [], [], [], []
    for l in range(DEPTH):
        lw = (w_in[l], b_in[l], conv_w[l], attn_sinks[l], w_attn_br[l], w_conv_br[l], w_o[l],
              ln1_g[l], ln1_b[l], w_router[l], b_router[l], w_gu[l], b_gu[l], w_d[l], b_d[l],
              ln2_g[l], ln2_b[l])
        if l < DEPTH - 1:
            zk1 = jnp.zeros((1, WINDOW, N_KV, HEAD_DIM), hm.dtype)
            zb1 = jnp.zeros((1, CONV_W - 1, D_CONV), hm.dtype)
            hm_next, km, vm, um = layer_forward(hm, None, None, zk1, zk1, False, zb1, *lw)
        else:
            km, vm, um = kv_and_conv_input(hm @ w_in[l][:, OFF_K:] + b_in[l][OFF_K:])
        km_b = jnp.broadcast_to(km, (b_p, N_META, N_KV, HEAD_DIM))
        vm_b = jnp.broadcast_to(vm, (b_p, N_META, N_KV, HEAD_DIM))
        buf_p = jnp.broadcast_to(um[:, N_META - (CONV_W - 1):], (b_p, CONV_W - 1, D_CONV))
        zk = jnp.zeros((b_p, WINDOW, N_KV, HEAD_DIM), hp.dtype)
        hp, kp, vp, up = layer_forward(hp, km_b, vm_b, zk, zk, False, buf_p, *lw)
        hs, ks_, vs_, us_ = layer_forward(hs, cache_meta_k[l], cache_meta_v[l], cache_win_k[l], cache_win_v[l],
                                          True, state_conv[l], *lw)
        mk_p.append(km_b)
        mv_p.append(vm_b)
        wk_p.append(kp[:, -WINDOW:])
        wv_p.append(vp[:, -WINDOW:])
        cv_p.append(up[:, -(CONV_W - 1):])
        wk_s.append(jnp.concatenate([cache_win_k[l].astype(ks_.dtype), ks_], axis=1)[:, -WINDOW:])
        wv_s.append(jnp.concatenate([cache_win_v[l].astype(vs_.dtype), vs_], axis=1)[:, -WINDOW:])
        cv_s.append(jnp.concatenate([state_conv[l].astype(us_.dtype), us_], axis=1)[:, -(CONV_W - 1):])
        if l < DEPTH - 1:
            hm = hm_next
    return (hp, hs, jnp.stack(mk_p), jnp.stack(mv_p), jnp.stack(wk_p), jnp.stack(wv_p), jnp.stack(cv_p),
            jnp.stack(wk_s), jnp.stack(wv_s), jnp.stack(cv_s))
```

```python
import functools

import jax
import jax.numpy as jnp
from jax import lax
from jax.experimental import pallas as pl
from jax.experimental.pallas import tpu as pltpu

D_MODEL = 1024
CHUNK = 64
N_META = 16
N_Q = 16
N_KV = 4
GROUP = N_Q // N_KV
HEAD_DIM = 64
ATTN_W = N_Q * HEAD_DIM
KV_W = N_KV * HEAD_DIM
WINDOW = 128
D_CONV = D_MODEL
CONV_W = 3
N_EXPERTS = 32
TOP_K = 4
D_FF = D_MODEL
SWIGLU_LIMIT = 7.0
SWIGLU_ALPHA = 1.702
EXPERT_BLOCK = 256
LN_EPS = 1e-5
DEPTH = 1
ALPHA = (2 * DEPTH) ** 0.25
ATTN_SCALE = HEAD_DIM ** -0.5

OFF_GA = ATTN_W
OFF_GC = OFF_GA + D_MODEL
OFF_CB = OFF_GC + D_MODEL
OFF_K = OFF_CB + D_CONV
OFF_V = OFF_K + KV_W
OFF_CC = OFF_V + KV_W
OFF_CH = OFF_CC + D_CONV
IN_COLS = OFF_CH + D_CONV

SUBLANES = 8
LANES = 128
ROW_TILES = D_MODEL // LANES
VMEM_LIMIT = 56 * 1024 * 1024

F32 = jnp.float32
BF16 = jnp.bfloat16


def _ln(x, g, b):
    mu = jnp.mean(x, axis=-1, keepdims=True)
    xc = x - mu
    var = jnp.mean(xc * xc, axis=-1, keepdims=True)
    return (xc * lax.rsqrt(var + LN_EPS)) * g + b


def _const_spec(shape):
    nd = len(shape)
    return pl.BlockSpec(shape, lambda *_: (0,) * nd, pipeline_mode=pl.Buffered(1))


def _inproj_kernel(x_ref, cbuf_ref, w_ref, b_ref, cw_ref, g_ref, be_ref,
                   q_ref, ga_ref, gc_ref, oc_ref, k_ref, v_ref, ul_ref, carry_ref):
    tm = x_ref.shape[1]

    @pl.when(pl.program_id(1) == 0)
    def _():
        carry_ref[...] = cbuf_ref[0]

    h = _ln(x_ref[0], g_ref[...], be_ref[...]).astype(BF16)

    def proj(off, width):
        return (jnp.dot(h, w_ref[:, off:off + width], preferred_element_type=F32)
                + b_ref[:, off:off + width])

    q_ref[0] = (proj(0, ATTN_W) * ATTN_SCALE).astype(BF16)
    ga_ref[0] = jax.nn.sigmoid(proj(OFF_GA, D_MODEL)).astype(BF16)
    gc_ref[0] = jax.nn.sigmoid(proj(OFF_GC, D_MODEL)).astype(BF16)
    k_ref[0] = proj(OFF_K, KV_W)
    v_ref[0] = proj(OFF_V, KV_W)

    u = proj(OFF_CC, D_CONV) * proj(OFF_CH, D_CONV)
    prev = carry_ref[...]
    row = lax.broadcasted_iota(jnp.int32, (tm, D_CONV), 0)
    u1 = jnp.where(row == 0, prev[1:2], pltpu.roll(u, 1, 0))
    u2 = jnp.where(row == 0, prev[0:1],
                   jnp.where(row == 1, prev[1:2], pltpu.roll(u, 2, 0)))
    conv = cw_ref[0:1] * u2 + cw_ref[1:2] * u1 + cw_ref[2:3] * u
    oc_ref[0] = (proj(OFF_CB, D_CONV) * conv).astype(BF16)
    last = u[tm - (CONV_W - 1):tm]
    carry_ref[...] = last
    ul_ref[0] = last


def _inproj(x, cbuf, w_in, b_in, conv_w, ln_g, ln_b, tm):
    bsz, t, _ = x.shape
    cb_map = (lambda b, i: (b, 0, 0)) if cbuf.shape[0] == bsz else (lambda b, i: (0, 0, 0))
    row_spec = lambda w: pl.BlockSpec((1, tm, w), lambda b, i: (b, i, 0))
    outs = (
        jax.ShapeDtypeStruct((bsz, t, ATTN_W), BF16),
        jax.ShapeDtypeStruct((bsz, t, D_MODEL), BF16),
        jax.ShapeDtypeStruct((bsz, t, D_MODEL), BF16),
        jax.ShapeDtypeStruct((bsz, t, D_CONV), BF16),
        jax.ShapeDtypeStruct((bsz, t, KV_W), F32),
        jax.ShapeDtypeStruct((bsz, t, KV_W), F32),
        jax.ShapeDtypeStruct((bsz, CONV_W - 1, D_CONV), F32),
    )
    return pl.pallas_call(
        _inproj_kernel,
        out_shape=outs,
        grid=(bsz, t // tm),
        in_specs=[
            row_spec(D_MODEL),
            pl.BlockSpec((1, CONV_W - 1, D_CONV), cb_map),
            _const_spec((D_MODEL, IN_COLS)),
            _const_spec((1, IN_COLS)),
            _const_spec((CONV_W, D_CONV)),
            _const_spec((1, D_MODEL)),
            _const_spec((1, D_MODEL)),
        ],
        out_specs=(
            row_spec(ATTN_W), row_spec(D_MODEL), row_spec(D_MODEL), row_spec(D_CONV),
            row_spec(KV_W), row_spec(KV_W),
            pl.BlockSpec((1, CONV_W - 1, D_CONV), lambda b, i: (b, 0, 0)),
        ),
        scratch_shapes=[pltpu.VMEM((CONV_W - 1, D_CONV), F32)],
        compiler_params=pltpu.CompilerParams(
            dimension_semantics=("arbitrary", "arbitrary"), vmem_limit_bytes=VMEM_LIMIT),
        name="inproj",
    )(x, cbuf, w_in, b_in, conv_w, ln_g, ln_b)


def _attn_kernel(sink_ref, q_ref, k_ref, v_ref, mk_ref, mv_ref, o_ref, *, past_valid, t_valid):
    tq = q_ref.shape[1]
    nchunk = tq // CHUNK
    nkeys = WINDOW + CHUNK
    tile = pl.program_id(1)
    mk = mk_ref[0]
    mv = mv_ref[0]
    contract = (((1,), (1,)), ((), ()))

    def chunk_body(ci, carry):
        row0 = pl.multiple_of((tile * nchunk + ci) * CHUNK, CHUNK)
        kc = k_ref[0, pl.ds(row0, nkeys), :]
        vc = v_ref[0, pl.ds(row0, nkeys), :]
        qc = q_ref[0, pl.ds(pl.multiple_of(ci * CHUNK, CHUNK), CHUNK), :]
        kpos = row0 + lax.broadcasted_iota(jnp.int32, (1, nkeys), 1)
        valid = kpos < WINDOW + t_valid
        if not past_valid:
            valid = jnp.logical_and(valid, kpos >= WINDOW)
        for g in range(N_KV):
            heads = [qc[:, (g * GROUP + i) * HEAD_DIM:(g * GROUP + i + 1) * HEAD_DIM]
                     for i in range(GROUP)]
            qg = jnp.concatenate(heads, axis=0)
            ksl = slice(g * HEAD_DIM, (g + 1) * HEAD_DIM)
            s = lax.dot_general(qg, kc[:, ksl], contract, preferred_element_type=F32)
            s = jnp.where(valid, s, -jnp.inf)
            sm = lax.dot_general(qg, mk[:, ksl], contract, preferred_element_type=F32)
            qrow = lax.broadcasted_iota(jnp.int32, (GROUP * CHUNK, 1), 0)
            sink = jnp.full((GROUP * CHUNK, 1), sink_ref[g * GROUP], F32)
            for i in range(1, GROUP):
                sink = jnp.where(qrow >= i * CHUNK, sink_ref[g * GROUP + i], sink)
            m = jnp.maximum(jnp.maximum(jnp.max(s, axis=-1, keepdims=True),
                                        jnp.max(sm, axis=-1, keepdims=True)), sink)
            p = jnp.exp(s - m)
            pm = jnp.exp(sm - m)
            den = (jnp.sum(p, axis=-1, keepdims=True) + jnp.exp(sink - m)
                   + jnp.sum(pm, axis=-1, keepdims=True))
            o = (jnp.dot((p / den).astype(BF16), vc[:, ksl], preferred_element_type=F32)
                 + jnp.dot((pm / den).astype(BF16), mv[:, ksl], preferred_element_type=F32))
            for i in range(GROUP):
                h = g * GROUP + i
                o_ref[0, pl.ds(pl.multiple_of(ci * CHUNK, CHUNK), CHUNK),
                      h * HEAD_DIM:(h + 1) * HEAD_DIM] = o[i * CHUNK:(i + 1) * CHUNK].astype(BF16)
        return carry

    lax.fori_loop(0, nchunk, chunk_body, 0)


def _attention(q, kfull, vfull, mk, mv, sinks, past_valid, t_valid, tq):
    bsz, t, _ = q.shape
    tk = kfull.shape[1]
    m_map = (lambda b, i: (b, 0, 0)) if mk.shape[0] == bsz else (lambda b, i: (0, 0, 0))
    return pl.pallas_call(
        functools.partial(_attn_kernel, past_valid=past_valid, t_valid=t_valid),
        out_shape=jax.ShapeDtypeStruct((bsz, t, ATTN_W), BF16),
        grid=(bsz, t // tq),
        in_specs=[
            pl.BlockSpec(memory_space=pltpu.SMEM),
            pl.BlockSpec((1, tq, ATTN_W), lambda b, i: (b, i, 0)),
            pl.BlockSpec((1, tk, KV_W), lambda b, i: (b, 0, 0)),
            pl.BlockSpec((1, tk, KV_W), lambda b, i: (b, 0, 0)),
            pl.BlockSpec((1, N_META, KV_W), m_map),
            pl.BlockSpec((1, N_META, KV_W), m_map),
        ],
        out_specs=pl.BlockSpec((1, tq, ATTN_W), lambda b, i: (b, i, 0)),
        compiler_params=pltpu.CompilerParams(
            dimension_semantics=("arbitrary", "arbitrary"), vmem_limit_bytes=VMEM_LIMIT),
        name="attn",
    )(sinks, q, kfull, vfull, mk, mv)


def _mix_kernel(x_ref, oa_ref, oc_ref, ga_ref, gc_ref, wa_ref, wc_ref, wo_ref,
                gin_ref, bin_ref, g1_ref, b1_ref, wr_ref, br_ref,
                h1_ref, idx_ref, gate_ref, cnt_ref):
    tm = x_ref.shape[0]

    @pl.when(pl.program_id(0) == 0)
    def _():
        cnt_ref[...] = jnp.zeros_like(cnt_ref)

    h0 = _ln(x_ref[...], gin_ref[...], bin_ref[...])
    a = jnp.dot(oa_ref[...], wa_ref[...], preferred_element_type=F32)
    c = jnp.dot(oc_ref[...], wc_ref[...], preferred_element_type=F32)
    mixed = ga_ref[...].astype(F32) * a + gc_ref[...].astype(F32) * c
    y = jnp.dot(mixed.astype(BF16), wo_ref[...], preferred_element_type=F32)
    h1 = _ln(ALPHA * h0 + y, g1_ref[...], b1_ref[...])
    h1_ref[...] = h1

    logits = lax.dot_general(wr_ref[...], h1.astype(BF16), (((1,), (1,)), ((), ())),
                             preferred_element_type=F32) + br_ref[...]
    e_iota = lax.broadcasted_iota(jnp.int32, (N_EXPERTS, tm), 0)
    vals, idxs = [], []
    onehot = jnp.zeros((N_EXPERTS, tm), F32)
    for _ in range(TOP_K):
        mx = jnp.max(logits, axis=0, keepdims=True)
        ix = jnp.min(jnp.where(logits == mx, e_iota, N_EXPERTS), axis=0, keepdims=True)
        sel = e_iota == ix
        vals.append(mx)
        idxs.append(ix)
        onehot = onehot + sel.astype(F32)
        logits = jnp.where(sel, -jnp.inf, logits)
    ev = [jnp.exp(v - vals[0]) for v in vals]
    den = ev[0] + ev[1] + ev[2] + ev[3]
    idx_ref[...] = jnp.concatenate(idxs, axis=0)
    gate_ref[...] = jnp.concatenate([e / den for e in ev], axis=0)
    cnt_ref[...] += jnp.sum(onehot, axis=1, keepdims=True)


def _mix(x, oa, oc, ga, gc, wa, wc, wo, gin, bin_, g1, b1, wr_t, br, tm):
    n = x.shape[0]
    row = lambda w: pl.BlockSpec((tm, w), lambda i: (i, 0))
    outs = (
        jax.ShapeDtypeStruct((n, D_MODEL), F32),
        jax.ShapeDtypeStruct((TOP_K, n), jnp.int32),
        jax.ShapeDtypeStruct((TOP_K, n), F32),
        jax.ShapeDtypeStruct((N_EXPERTS, LANES), F32),
    )
    return pl.pallas_call(
        _mix_kernel,
        out_shape=outs,
        grid=(n // tm,),
        in_specs=[
            row(D_MODEL), row(ATTN_W), row(D_CONV), row(D_MODEL), row(D_MODEL),
            _const_spec((ATTN_W, D_MODEL)), _const_spec((D_CONV, D_MODEL)),
            _const_spec((D_MODEL, D_MODEL)),
            _const_spec((1, D_MODEL)), _const_spec((1, D_MODEL)),
            _const_spec((1, D_MODEL)), _const_spec((1, D_MODEL)),
            _const_spec((N_EXPERTS, D_MODEL)), _const_spec((N_EXPERTS, 1)),
        ],
        out_specs=(
            row(D_MODEL),
            pl.BlockSpec((TOP_K, tm), lambda i: (0, i)),
            pl.BlockSpec((TOP_K, tm), lambda i: (0, i)),
            pl.BlockSpec((N_EXPERTS, LANES), lambda i: (0, 0)),
        ),
        compiler_params=pltpu.CompilerParams(
            dimension_semantics=("arbitrary",), vmem_limit_bytes=VMEM_LIMIT),
        name="mix",
    )(x, oa, oc, ga, gc, wa, wc, wo, gin, bin_, g1, b1, wr_t, br)


def _plan_kernel(idx_ref, poff_ref, dest_ref, run_ref):
    tm = idx_ref.shape[1]

    @pl.when(pl.program_id(0) == 0)
    def _():
        run_ref[...] = jnp.zeros_like(run_ref)

    e_iota = lax.broadcasted_iota(jnp.int32, (N_EXPERTS, tm), 0)
    sels = [e_iota == idx_ref[k:k + 1, :] for k in range(TOP_K)]
    oh = sels[0].astype(F32)
    for k in range(1, TOP_K):
        oh = oh + sels[k].astype(F32)
    tri = (lax.broadcasted_iota(jnp.int32, (tm, tm), 0)
           < lax.broadcasted_iota(jnp.int32, (tm, tm), 1)).astype(BF16)
    excl = jnp.dot(oh.astype(BF16), tri, preferred_element_type=F32)
    pos = excl + (run_ref[:, 0:1] + poff_ref[:, 0:1])
    dest = [jnp.sum(jnp.where(s, pos, 0.0), axis=0, keepdims=True) for s in sels]
    dest_ref[...] = jnp.concatenate(dest, axis=0).astype(jnp.int32)
    run_ref[...] += jnp.sum(oh, axis=1, keepdims=True)


def _plan(idx, poff, tm):
    n = idx.shape[1]
    return pl.pallas_call(
        _plan_kernel,
        out_shape=jax.ShapeDtypeStruct((TOP_K, n), jnp.int32),
        grid=(n // tm,),
        in_specs=[pl.BlockSpec((TOP_K, tm), lambda i: (0, i)),
                  _const_spec((N_EXPERTS, LANES))],
        out_specs=pl.BlockSpec((TOP_K, tm), lambda i: (0, i)),
        scratch_shapes=[pltpu.VMEM((N_EXPERTS, LANES), F32)],
        compiler_params=pltpu.CompilerParams(dimension_semantics=("arbitrary",)),
        name="plan",
    )(idx, poff)


def _row_copies(src_of, dst_of, idx_smem, sem, tm):
    def body(j, fn):
        for k in range(TOP_K):
            r = idx_smem[k * tm + j]
            fn(pltpu.make_async_copy(src_of(j, k, r), dst_of(j, k, r), sem))
    return body


def _dispatch_kernel(dest_hbm, h_hbm, xg_in, xg_hbm, idx_smem, isem, sem):
    del xg_in
    tm = idx_smem.shape[0] // TOP_K
    i = pl.program_id(0)
    cp = pltpu.make_async_copy(dest_hbm.at[i], idx_smem, isem)
    cp.start()
    cp.wait()
    base = i * tm
    body = _row_copies(lambda j, k, r: h_hbm.at[base + j], lambda j, k, r: xg_hbm.at[r],
                       idx_smem, sem, tm)

    def start(j, c):
        body(j, lambda d: d.start())
        return c

    def wait(j, c):
        body(j, lambda d: d.wait())
        return c

    lax.fori_loop(0, tm, start, 0)
    lax.fori_loop(0, tm, wait, 0)


def _dispatch(dest_tiles, h3, xg_zero, tm):
    ntile = dest_tiles.shape[0]
    return pl.pallas_call(
        _dispatch_kernel,
        out_shape=jax.ShapeDtypeStruct(xg_zero.shape, xg_zero.dtype),
        grid=(ntile,),
        in_specs=[pl.BlockSpec(memory_space=pl.ANY)] * 3,
        out_specs=pl.BlockSpec(memory_space=pl.ANY),
        scratch_shapes=[pltpu.SMEM((TOP_K * tm,), jnp.int32),
                        pltpu.SemaphoreType.DMA, pltpu.SemaphoreType.DMA],
        input_output_aliases={2: 0},
        compiler_params=pltpu.CompilerParams(dimension_semantics=("arbitrary",)),
        name="dispatch",
    )(dest_tiles, h3, xg_zero)


def _ffn_kernel(be_ref, nu_ref, x_ref, wgu_ref, bgu_ref, wd_ref, bd_ref, y_ref):
    del be_ref

    @pl.when(pl.program_id(0) < nu_ref[0])
    def _():
        x = jnp.concatenate([x_ref[:, s, :] for s in range(ROW_TILES)], axis=-1)
        hgu = jnp.dot(x.astype(BF16), wgu_ref[0], preferred_element_type=F32) + bgu_ref[0]
        g = jnp.minimum(hgu[:, :D_FF], SWIGLU_LIMIT)
        up = jnp.clip(hgu[:, D_FF:], -SWIGLU_LIMIT, SWIGLU_LIMIT)
        act = (up + 1.0) * (g * jax.nn.sigmoid(g * SWIGLU_ALPHA))
        y = jnp.dot(act.astype(BF16), wd_ref[0], preferred_element_type=F32) + bd_ref[0]
        for s in range(ROW_TILES):
            y_ref[:, s, :] = y[:, s * LANES:(s + 1) * LANES]


def _ffn(block_e, n_used, xg, wgu, bgu, wd, bd):
    nb = xg.shape[0] // EXPERT_BLOCK
    blk = lambda i, be, nu: (jnp.minimum(i, nu[0] - 1), 0, 0)
    wsel = lambda i, be, nu: (be[jnp.minimum(i, nu[0] - 1)], 0, 0)
    grid_spec = pltpu.PrefetchScalarGridSpec(
        num_scalar_prefetch=2,
        grid=(nb,),
        in_specs=[
            pl.BlockSpec((EXPERT_BLOCK, ROW_TILES, LANES), blk),
            pl.BlockSpec((1, D_MODEL, 2 * D_FF), wsel),
            pl.BlockSpec((1, 1, 2 * D_FF), wsel),
            pl.BlockSpec((1, D_FF, D_MODEL), wsel),
            pl.BlockSpec((1, 1, D_MODEL), wsel),
        ],
        out_specs=pl.BlockSpec((EXPERT_BLOCK, ROW_TILES, LANES), blk),
    )
    return pl.pallas_call(
        _ffn_kernel,
        out_shape=jax.ShapeDtypeStruct(xg.shape, F32),
        grid_spec=grid_spec,
        compiler_params=pltpu.CompilerParams(
            dimension_semantics=("arbitrary",), vmem_limit_bytes=VMEM_LIMIT),
        name="ffn",
    )(block_e, n_used, xg, wgu, bgu, wd, bd)


def _combine_kernel(dest_hbm, yb_hbm, gate_ref, h1_ref, g2_ref, b2_ref, out_ref,
                    idx_smem, gbuf, isem, sem):
    tm = h1_ref.shape[0]
    i = pl.program_id(0)
    cp = pltpu.make_async_copy(dest_hbm.at[i], idx_smem, isem)
    cp.start()
    cp.wait()
    body = _row_copies(lambda j, k, r: yb_hbm.at[r], lambda j, k, r: gbuf.at[k, j],
                       idx_smem, sem, tm)

    def start(j, c):
        body(j, lambda d: d.start())
        return c

    def wait(j, c):
        body(j, lambda d: d.wait())
        return c

    lax.fori_loop(0, tm, start, 0)
    lax.fori_loop(0, tm, wait, 0)

    acc = ALPHA * h1_ref[...]
    for k in range(TOP_K):
        yk = jnp.concatenate([gbuf[k, :, s, :] for s in range(ROW_TILES)], axis=-1)
        acc = acc + gate_ref[:, k:k + 1] * yk
    out_ref[...] = _ln(acc, g2_ref[...], b2_ref[...])


def _combine(dest_tiles, ybuf, gate_t, h1, g2, b2, tm):
    n = h1.shape[0]
    return pl.pallas_call(
        _combine_kernel,
        out_shape=jax.ShapeDtypeStruct((n, D_MODEL), F32),
        grid=(n // tm,),
        in_specs=[
            pl.BlockSpec(memory_space=pl.ANY),
            pl.BlockSpec(memory_space=pl.ANY),
            pl.BlockSpec((tm, TOP_K), lambda i: (i, 0)),
            pl.BlockSpec((tm, D_MODEL), lambda i: (i, 0)),
            _const_spec((1, D_MODEL)), _const_spec((1, D_MODEL)),
        ],
        out_specs=pl.BlockSpec((tm, D_MODEL), lambda i: (i, 0)),
        scratch_shapes=[pltpu.SMEM((TOP_K * tm,), jnp.int32),
                        pltpu.VMEM((TOP_K, tm, ROW_TILES, LANES), F32),
                        pltpu.SemaphoreType.DMA, pltpu.SemaphoreType.DMA],
        compiler_params=pltpu.CompilerParams(
            dimension_semantics=("arbitrary",), vmem_limit_bytes=VMEM_LIMIT),
        name="combine",
    )(dest_tiles, ybuf, gate_t, h1, g2, b2)


def _pick_tile(t, pref):
    return pref if t % pref == 0 else t


def kernel(x_prompt, x_sample, cache_meta_k, cache_meta_v, cache_win_k, cache_win_v, state_conv,
           meta_tokens, ln_in_g, ln_in_b, w_in, b_in, conv_w, attn_sinks, w_attn_br, w_conv_br, w_o,
           ln1_g, ln1_b, w_router, b_router, w_gu, b_gu, w_d, b_d, ln2_g, ln2_b):
    bp, tp, _ = x_prompt.shape
    bs, ts, _ = x_sample.shape
    row2 = lambda a: a.reshape(1, -1)

    w_in_b = w_in[0].astype(BF16)
    b_in_r = row2(b_in[0])
    gin, bin_ = row2(ln_in_g), row2(ln_in_b)
    wa, wc, wo = w_attn_br[0].astype(BF16), w_conv_br[0].astype(BF16), w_o[0].astype(BF16)
    wr_t = w_router[0].T.astype(BF16)
    br = b_router[0].reshape(N_EXPERTS, 1)
    wgu = jnp.concatenate([w_gu[0][..., 0::2], w_gu[0][..., 1::2]], axis=-1).astype(BF16)
    bgu = jnp.concatenate([b_gu[0][..., 0::2], b_gu[0][..., 1::2]], axis=-1)[:, None, :]
    wd = w_d[0].astype(BF16)
    bd = b_d[0][:, None, :]
    sinks = attn_sinks[0]

    zbuf = jnp.zeros((1, CONV_W - 1, D_CONV), F32)
    _, _, _, _, km, vm, um_last = _inproj(meta_tokens[None], zbuf, w_in_b, b_in_r, conv_w[0],
                                          gin, bin_, N_META)

    def stream(x, cbuf, past_k, past_v, mk, mv, past_valid, tm_pref):
        bsz, t, _ = x.shape
        q, ga, gc, oc, k, v, u_last = _inproj(x, cbuf, w_in_b, b_in_r, conv_w[0], gin, bin_,
                                             _pick_tile(t, tm_pref))
        tpad = -(-t // CHUNK) * CHUNK
        padq = ((0, 0), (0, tpad - t), (0, 0))
        kfull = jnp.concatenate([past_k, jnp.pad(k, padq)], axis=1).astype(BF16)
        vfull = jnp.concatenate([past_v, jnp.pad(v, padq)], axis=1).astype(BF16)
        oa = _attention(jnp.pad(q, padq), kfull, vfull, mk.astype(BF16), mv.astype(BF16), sinks,
                        past_valid, t, _pick_tile(tpad, tm_pref))[:, :t]
        n = bsz * t
        flat = lambda a: a.reshape(n, a.shape[-1])
        h1, idx, gate, cnt = _mix(flat(x), flat(oa), flat(oc), flat(ga), flat(gc), wa, wc, wo,
                                  gin, bin_, row2(ln1_g[0]), row2(ln1_b[0]), wr_t, br,
                                  _pick_tile(n, tm_pref))
        return h1, idx, gate, cnt, k, v, u_last

    zk = jnp.zeros((bp, WINDOW, KV_W), F32)
    h1p, idxp, gatep, cntp, kp, vp, ulp = stream(x_prompt, um_last, zk, zk, km, vm, False, 512)
    wk_s = cache_win_k[0].reshape(bs, WINDOW, KV_W)
    wv_s = cache_win_v[0].reshape(bs, WINDOW, KV_W)
    h1s, idxs, gates, cnts, ks, vs, uls = stream(
        x_sample, state_conv[0], wk_s, wv_s, cache_meta_k[0].reshape(bs, N_META, KV_W),
        cache_meta_v[0].reshape(bs, N_META, KV_W), True, 512)

    h1 = jnp.concatenate([h1p, h1s], axis=0)
    idx = jnp.concatenate([idxp, idxs], axis=1)
    gate = jnp.concatenate([gatep, gates], axis=1)
    n = h1.shape[0]
    tmm = EXPERT_BLOCK
    counts = (cntp[:, 0] + cnts[:, 0]).astype(jnp.int32)
    padded = (counts + EXPERT_BLOCK - 1) // EXPERT_BLOCK * EXPERT_BLOCK
    pend = jnp.cumsum(padded)
    poff = pend - padded
    nb = -(-(n * TOP_K) // EXPERT_BLOCK) + N_EXPERTS
    block_e = jnp.minimum(jnp.searchsorted(pend, jnp.arange(nb) * EXPERT_BLOCK, side='right'),
                          N_EXPERTS - 1).astype(jnp.int32)
    n_used = (pend[-1:] // EXPERT_BLOCK).astype(jnp.int32)
    poff_b = jnp.broadcast_to(poff.astype(F32)[:, None], (N_EXPERTS, LANES))

    dest = _plan(idx, poff_b, tmm)
    dest_tiles = dest.reshape(TOP_K, n // tmm, tmm).transpose(1, 0, 2).reshape(n // tmm, TOP_K * tmm)
    h3 = h1.reshape(n, ROW_TILES, LANES)
    xg = _dispatch(dest_tiles, h3, jnp.zeros((nb * EXPERT_BLOCK, ROW_TILES, LANES), F32), tmm)
    ybuf = _ffn(block_e, n_used, xg, wgu, bgu, wd, bd)
    y = _combine(dest_tiles, ybuf, gate.T, h1, row2(ln2_g[0]), row2(ln2_b[0]), tmm)

    y_prompt = y[:bp * tp].reshape(bp, tp, D_MODEL)
    y_sample = y[bp * tp:].reshape(bs, ts, D_MODEL)
    kv5 = lambda a: a.reshape(a.shape[0], a.shape[1], N_KV, HEAD_DIM)[None]
    mk_p = jnp.broadcast_to(kv5(km), (1, bp, N_META, N_KV, HEAD_DIM))
    mv_p = jnp.broadcast_to(kv5(vm), (1, bp, N_META, N_KV, HEAD_DIM))
    wk_p = kv5(kp[:, -WINDOW:])
    wv_p = kv5(vp[:, -WINDOW:])
    wk_o = kv5(jnp.concatenate([wk_s, ks], axis=1)[:, -WINDOW:])
    wv_o = kv5(jnp.concatenate([wv_s, vs], axis=1)[:, -WINDOW:])
    return (y_prompt, y_sample, mk_p, mv_p, wk_p, wv_p, ulp[None], wk_o, wv_o, uls[None])
```

```python
import functools

import jax
import jax.numpy as jnp
from jax import lax
from jax.experimental import pallas as pl
from jax.experimental.pallas import tpu as pltpu

D_MODEL = 1024
CHUNK = 64
N_META = 16
N_Q = 16
N_KV = 4
GROUP = N_Q // N_KV
HEAD_DIM = 64
ATTN_W = N_Q * HEAD_DIM
KV_W = N_KV * HEAD_DIM
WINDOW = 128
D_CONV = D_MODEL
CONV_W = 3
N_EXPERTS = 32
TOP_K = 4
D_FF = D_MODEL
SWIGLU_LIMIT = 7.0
SWIGLU_ALPHA = 1.702
EXPERT_BLOCK = 256
LN_EPS = 1e-5
DEPTH = 1
ALPHA = (2 * DEPTH) ** 0.25
ATTN_SCALE = HEAD_DIM ** -0.5

OFF_GA = ATTN_W
OFF_GC = OFF_GA + D_MODEL
OFF_CB = OFF_GC + D_MODEL
OFF_K = OFF_CB + D_CONV
OFF_V = OFF_K + KV_W
OFF_CC = OFF_V + KV_W
OFF_CH = OFF_CC + D_CONV
IN_COLS = OFF_CH + D_CONV

SUBLANES = 8
LANES = 128
ROW_TILES = D_MODEL // LANES
VMEM_LIMIT = 56 * 1024 * 1024

F32 = jnp.float32
BF16 = jnp.bfloat16


def _ln(x, g, b):
    mu = jnp.mean(x, axis=-1, keepdims=True)
    xc = x - mu
    var = jnp.mean(xc * xc, axis=-1, keepdims=True)
    return (xc * lax.rsqrt(var + LN_EPS)) * g + b


def _const_spec(shape):
    nd = len(shape)
    return pl.BlockSpec(shape, lambda *_: (0,) * nd, pipeline_mode=pl.Buffered(1))


def _inproj_kernel(x_ref, cbuf_ref, w_ref, b_ref, cw_ref, g_ref, be_ref,
                   q_ref, ga_ref, gc_ref, oc_ref, k_ref, v_ref, ul_ref, carry_ref):
    tm = x_ref.shape[1]

    @pl.when(pl.program_id(1) == 0)
    def _():
        carry_ref[...] = cbuf_ref[0]

    h = _ln(x_ref[0], g_ref[...], be_ref[...]).astype(BF16)

    def proj(off, width):
        return (jnp.dot(h, w_ref[:, off:off + width], preferred_element_type=F32)
                + b_ref[:, off:off + width])

    q_ref[0] = (proj(0, ATTN_W) * ATTN_SCALE).astype(BF16)
    ga_ref[0] = jax.nn.sigmoid(proj(OFF_GA, D_MODEL)).astype(BF16)
    gc_ref[0] = jax.nn.sigmoid(proj(OFF_GC, D_MODEL)).astype(BF16)
    k_ref[0] = proj(OFF_K, KV_W)
    v_ref[0] = proj(OFF_V, KV_W)

    u = proj(OFF_CC, D_CONV) * proj(OFF_CH, D_CONV)
    prev = carry_ref[...]
    row = lax.broadcasted_iota(jnp.int32, (tm, D_CONV), 0)
    u1 = jnp.where(row == 0, prev[1:2], pltpu.roll(u, 1, 0))
    u2 = jnp.where(row == 0, prev[0:1],
                   jnp.where(row == 1, prev[1:2], pltpu.roll(u, 2, 0)))
    conv = cw_ref[0:1] * u2 + cw_ref[1:2] * u1 + cw_ref[2:3] * u
    oc_ref[0] = (proj(OFF_CB, D_CONV) * conv).astype(BF16)
    last = u[tm - (CONV_W - 1):tm]
    carry_ref[...] = last
    ul_ref[0] = last


def _inproj(x, cbuf, w_in, b_in, conv_w, ln_g, ln_b, tm):
    bsz, t, _ = x.shape
    cb_map = (lambda b, i: (b, 0, 0)) if cbuf.shape[0] == bsz else (lambda b, i: (0, 0, 0))
    row_spec = lambda w: pl.BlockSpec((1, tm, w), lambda b, i: (b, i, 0))
    outs = (
        jax.ShapeDtypeStruct((bsz, t, ATTN_W), BF16),
        jax.ShapeDtypeStruct((bsz, t, D_MODEL), BF16),
        jax.ShapeDtypeStruct((bsz, t, D_MODEL), BF16),
        jax.ShapeDtypeStruct((bsz, t, D_CONV), BF16),
        jax.ShapeDtypeStruct((bsz, t, KV_W), F32),
        jax.ShapeDtypeStruct((bsz, t, KV_W), F32),
        jax.ShapeDtypeStruct((bsz, CONV_W - 1, D_CONV), F32),
    )
    return pl.pallas_call(
        _inproj_kernel,
        out_shape=outs,
        grid=(bsz, t // tm),
        in_specs=[
            row_spec(D_MODEL),
            pl.BlockSpec((1, CONV_W - 1, D_CONV), cb_map),
            _const_spec((D_MODEL, IN_COLS)),
            _const_spec((1, IN_COLS)),
            _const_spec((CONV_W, D_CONV)),
            _const_spec((1, D_MODEL)),
            _const_spec((1, D_MODEL)),
        ],
        out_specs=(
            row_spec(ATTN_W), row_spec(D_MODEL), row_spec(D_MODEL), row_spec(D_CONV),
            row_spec(KV_W), row_spec(KV_W),
            pl.BlockSpec((1, CONV_W - 1, D_CONV), lambda b, i: (b, 0, 0)),
        ),
        scratch_shapes=[pltpu.VMEM((CONV_W - 1, D_CONV), F32)],
        compiler_params=pltpu.CompilerParams(
            dimension_semantics=("arbitrary", "arbitrary"), vmem_limit_bytes=VMEM_LIMIT),
        name="inproj",
    )(x, cbuf, w_in, b_in, conv_w, ln_g, ln_b)


def _attn_kernel(sink_ref, q_ref, k_ref, v_ref, mk_ref, mv_ref, o_ref, *, past_valid, t_valid):
    tq = q_ref.shape[1]
    nchunk = tq // CHUNK
    nkeys = WINDOW + CHUNK
    tile = pl.program_id(1)
    mk = mk_ref[0]
    mv = mv_ref[0]
    contract = (((1,), (1,)), ((), ()))

    def chunk_body(ci, carry):
        row0 = pl.multiple_of((tile * nchunk + ci) * CHUNK, CHUNK)
        kc = k_ref[0, pl.ds(row0, nkeys), :]
        vc = v_ref[0, pl.ds(row0, nkeys), :]
        qc = q_ref[0, pl.ds(pl.multiple_of(ci * CHUNK, CHUNK), CHUNK), :]
        kpos = row0 + lax.broadcasted_iota(jnp.int32, (1, nkeys), 1)
        valid = kpos < WINDOW + t_valid
        if not past_valid:
            valid = jnp.logical_and(valid, kpos >= WINDOW)
        for g in range(N_KV):
            heads = [qc[:, (g * GROUP + i) * HEAD_DIM:(g * GROUP + i + 1) * HEAD_DIM]
                     for i in range(GROUP)]
            qg = jnp.concatenate(heads, axis=0)
            ksl = slice(g * HEAD_DIM, (g + 1) * HEAD_DIM)
            s = lax.dot_general(qg, kc[:, ksl], contract, preferred_element_type=F32)
            s = jnp.where(valid, s, -jnp.inf)
            sm = lax.dot_general(qg, mk[:, ksl], contract, preferred_element_type=F32)
            qrow = lax.broadcasted_iota(jnp.int32, (GROUP * CHUNK, 1), 0)
            sink = jnp.full((GROUP * CHUNK, 1), sink_ref[g * GROUP], F32)
            for i in range(1, GROUP):
                sink = jnp.where(qrow >= i * CHUNK, sink_ref[g * GROUP + i], sink)
            m = jnp.maximum(jnp.maximum(jnp.max(s, axis=-1, keepdims=True),
                                        jnp.max(sm, axis=-1, keepdims=True)), sink)
            p = jnp.exp(s - m)
            pm = jnp.exp(sm - m)
            den = (jnp.sum(p, axis=-1, keepdims=True) + jnp.exp(sink - m)
                   + jnp.sum(pm, axis=-1, keepdims=True))
            o = (jnp.dot((p / den).astype(BF16), vc[:, ksl], preferred_element_type=F32)
                 + jnp.dot((pm / den).astype(BF16), mv[:, ksl], preferred_element_type=F32))
            for i in range(GROUP):
                h = g * GROUP + i
                o_ref[0, pl.ds(pl.multiple_of(ci * CHUNK, CHUNK), CHUNK),
                      h * HEAD_DIM:(h + 1) * HEAD_DIM] = o[i * CHUNK:(i + 1) * CHUNK].astype(BF16)
        return carry

    lax.fori_loop(0, nchunk, chunk_body, 0)


def _attention(q, kfull, vfull, mk, mv, sinks, past_valid, t_valid, tq):
    bsz, t, _ = q.shape
    tk = kfull.shape[1]
    m_map = (lambda b, i: (b, 0, 0)) if mk.shape[0] == bsz else (lambda b, i: (0, 0, 0))
    return pl.pallas_call(
        functools.partial(_attn_kernel, past_valid=past_valid, t_valid=t_valid),
        out_shape=jax.ShapeDtypeStruct((bsz, t, ATTN_W), BF16),
        grid=(bsz, t // tq),
        in_specs=[
            pl.BlockSpec(memory_space=pltpu.SMEM),
            pl.BlockSpec((1, tq, ATTN_W), lambda b, i: (b, i, 0)),
            pl.BlockSpec((1, tk, KV_W), lambda b, i: (b, 0, 0)),
            pl.BlockSpec((1, tk, KV_W), lambda b, i: (b, 0, 0)),
            pl.BlockSpec((1, N_META, KV_W), m_map),
            pl.BlockSpec((1, N_META, KV_W), m_map),
        ],
        out_specs=pl.BlockSpec((1, tq, ATTN_W), lambda b, i: (b, i, 0)),
        compiler_params=pltpu.CompilerParams(
            dimension_semantics=("arbitrary", "arbitrary"), vmem_limit_bytes=VMEM_LIMIT),
        name="attn",
    )(sinks, q, kfull, vfull, mk, mv)


def _mix_kernel(x_ref, oa_ref, oc_ref, ga_ref, gc_ref, wa_ref, wc_ref, wo_ref,
                gin_ref, bin_ref, g1_ref, b1_ref, wr_ref, br_ref,
                h1_ref, idx_ref, gate_ref, cnt_ref):
    tm = x_ref.shape[0]

    @pl.when(pl.program_id(0) == 0)
    def _():
        cnt_ref[...] = jnp.zeros_like(cnt_ref)

    h0 = _ln(x_ref[...], gin_ref[...], bin_ref[...])
    a = jnp.dot(oa_ref[...], wa_ref[...], preferred_element_type=F32)
    c = jnp.dot(oc_ref[...], wc_ref[...], preferred_element_type=F32)
    mixed = ga_ref[...].astype(F32) * a + gc_ref[...].astype(F32) * c
    y = jnp.dot(mixed.astype(BF16), wo_ref[...], preferred_element_type=F32)
    h1 = _ln(ALPHA * h0 + y, g1_ref[...], b1_ref[...])
    h1_ref[...] = h1

    logits = lax.dot_general(wr_ref[...], h1.astype(BF16), (((1,), (1,)), ((), ())),
                             preferred_element_type=F32) + br_ref[...]
    e_iota = lax.broadcasted_iota(jnp.int32, (N_EXPERTS, tm), 0)
    vals, idxs = [], []
    onehot = jnp.zeros((N_EXPERTS, tm), F32)
    for _ in range(TOP_K):
        mx = jnp.max(logits, axis=0, keepdims=True)
        ix = jnp.min(jnp.where(logits == mx, e_iota, N_EXPERTS), axis=0, keepdims=True)
        sel = e_iota == ix
        vals.append(mx)
        idxs.append(ix)
        onehot = onehot + sel.astype(F32)
        logits = jnp.where(sel, -jnp.inf, logits)
    ev = [jnp.exp(v - vals[0]) for v in vals]
    den = ev[0] + ev[1] + ev[2] + ev[3]
    idx_ref[...] = jnp.concatenate(idxs, axis=0)
    gate_ref[...] = jnp.concatenate([e / den for e in ev], axis=0)
    cnt_ref[...] += jnp.sum(onehot, axis=1, keepdims=True)


def _mix(x, oa, oc, ga, gc, wa, wc, wo, gin, bin_, g1, b1, wr_t, br, tm):
    n = x.shape[0]
    row = lambda w: pl.BlockSpec((tm, w), lambda i: (i, 0))
    outs = (
        jax.ShapeDtypeStruct((n, D_MODEL), F32),
        jax.ShapeDtypeStruct((TOP_K, n), jnp.int32),
        jax.ShapeDtypeStruct((TOP_K, n), F32),
        jax.ShapeDtypeStruct((N_EXPERTS, LANES), F32),
    )
    return pl.pallas_call(
        _mix_kernel,
        out_shape=outs,
        grid=(n // tm,),
        in_specs=[
            row(D_MODEL), row(ATTN_W), row(D_CONV), row(D_MODEL), row(D_MODEL),
            _const_spec((ATTN_W, D_MODEL)), _const_spec((D_CONV, D_MODEL)),
            _const_spec((D_MODEL, D_MODEL)),
            _const_spec((1, D_MODEL)), _const_spec((1, D_MODEL)),
            _const_spec((1, D_MODEL)), _const_spec((1, D_MODEL)),
            _const_spec((N_EXPERTS, D_MODEL)), _const_spec((N_EXPERTS, 1)),
        ],
        out_specs=(
            row(D_MODEL),
            pl.BlockSpec((TOP_K, tm), lambda i: (0, i)),
            pl.BlockSpec((TOP_K, tm), lambda i: (0, i)),
            pl.BlockSpec((N_EXPERTS, LANES), lambda i: (0, 0)),
        ),
        compiler_params=pltpu.CompilerParams(
            dimension_semantics=("arbitrary",), vmem_limit_bytes=VMEM_LIMIT),
        name="mix",
    )(x, oa, oc, ga, gc, wa, wc, wo, gin, bin_, g1, b1, wr_t, br)


def _plan_kernel(idx_ref, poff_ref, dest_ref, run_ref):
    tm = idx_ref.shape[1]

    @pl.when(pl.program_id(0) == 0)
    def _():
        run_ref[...] = jnp.zeros_like(run_ref)

    e_iota = lax.broadcasted_iota(jnp.int32, (N_EXPERTS, tm), 0)
    sels = [e_iota == idx_ref[k:k + 1, :] for k in range(TOP_K)]
    oh = sels[0].astype(F32)
    for k in range(1, TOP_K):
        oh = oh + sels[k].astype(F32)
    tri = (lax.broadcasted_iota(jnp.int32, (tm, tm), 0)
           < lax.broadcasted_iota(jnp.int32, (tm, tm), 1)).astype(BF16)
    excl = jnp.dot(oh.astype(BF16), tri, preferred_element_type=F32)
    pos = excl + (run_ref[:, 0:1] + poff_ref[:, 0:1])
    dest = [jnp.sum(jnp.where(s, pos, 0.0), axis=0, keepdims=True) for s in sels]
    dest_ref[...] = jnp.concatenate(dest, axis=0).astype(jnp.int32)
    run_ref[...] += jnp.sum(oh, axis=1, keepdims=True)


def _plan(idx, poff, tm):
    n = idx.shape[1]
    return pl.pallas_call(
        _plan_kernel,
        out_shape=jax.ShapeDtypeStruct((TOP_K, n), jnp.int32),
        grid=(n // tm,),
        in_specs=[pl.BlockSpec((TOP_K, tm), lambda i: (0, i)),
                  _const_spec((N_EXPERTS, LANES))],
        out_specs=pl.BlockSpec((TOP_K, tm), lambda i: (0, i)),
        scratch_shapes=[pltpu.VMEM((N_EXPERTS, LANES), F32)],
        compiler_params=pltpu.CompilerParams(dimension_semantics=("arbitrary",)),
        name="plan",
    )(idx, poff)


def _row_copies(src_of, dst_of, idx_smem, sem, tm):
    def body(j, fn):
        for k in range(TOP_K):
            r = idx_smem[k * tm + j]
            fn(pltpu.make_async_copy(src_of(j, k, r), dst_of(j, k, r), sem))
    return body


def _dispatch_kernel(dest_hbm, h_ref, xg_in, xg_hbm, idx_smem, isem, sem):
    del xg_in
    tm = h_ref.shape[0]
    i = pl.program_id(0)
    cp = pltpu.make_async_copy(dest_hbm.at[i], idx_smem, isem)
    cp.start()
    cp.wait()
    body = _row_copies(lambda j, k, r: h_ref.at[j], lambda j, k, r: xg_hbm.at[r],
                       idx_smem, sem, tm)

    def start(j, c):
        body(j, lambda d: d.start())
        return c

    def wait(j, c):
        body(j, lambda d: d.wait())
        return c

    lax.fori_loop(0, tm, start, 0)
    lax.fori_loop(0, tm, wait, 0)


def _dispatch(dest_tiles, h3, xg_zero, tm):
    ntile = dest_tiles.shape[0]
    return pl.pallas_call(
        _dispatch_kernel,
        out_shape=jax.ShapeDtypeStruct(xg_zero.shape, xg_zero.dtype),
        grid=(ntile,),
        in_specs=[pl.BlockSpec(memory_space=pl.ANY),
                  pl.BlockSpec((tm, ROW_TILES, LANES), lambda i: (i, 0, 0)),
                  pl.BlockSpec(memory_space=pl.ANY)],
        out_specs=pl.BlockSpec(memory_space=pl.ANY),
        scratch_shapes=[pltpu.SMEM((TOP_K * tm,), jnp.int32),
                        pltpu.SemaphoreType.DMA, pltpu.SemaphoreType.DMA],
        input_output_aliases={2: 0},
        compiler_params=pltpu.CompilerParams(dimension_semantics=("arbitrary",)),
        name="dispatch",
    )(dest_tiles, h3, xg_zero)


GU_GROUP = 2 * LANES


def _wprep_kernel(w_ref, o_ref):
    r = lax.broadcasted_iota(jnp.int32, (GU_GROUP, GU_GROUP), 0)
    c = lax.broadcasted_iota(jnp.int32, (GU_GROUP, GU_GROUP), 1)
    perm = (r == jnp.where(c < LANES, 2 * c, 2 * (c - LANES) + 1)).astype(BF16)
    for m in range(2 * D_FF // GU_GROUP):
        cols = slice(m * GU_GROUP, (m + 1) * GU_GROUP)
        o_ref[0, :, cols] = jnp.dot(w_ref[0, :, cols].astype(BF16), perm,
                                    preferred_element_type=F32).astype(BF16)


def _wprep(w_gu):
    ne = w_gu.shape[0]
    return pl.pallas_call(
        _wprep_kernel,
        out_shape=jax.ShapeDtypeStruct(w_gu.shape, BF16),
        grid=(ne,),
        in_specs=[pl.BlockSpec((1, D_MODEL, 2 * D_FF), lambda e: (e, 0, 0))],
        out_specs=pl.BlockSpec((1, D_MODEL, 2 * D_FF), lambda e: (e, 0, 0)),
        compiler_params=pltpu.CompilerParams(
            dimension_semantics=("arbitrary",), vmem_limit_bytes=VMEM_LIMIT),
        name="wprep",
    )(w_gu)


def _ffn_kernel(be_ref, nu_ref, x_ref, wgu_ref, bgu_ref, wd_ref, bd_ref, y_ref):
    del be_ref

    @pl.when(pl.program_id(0) < nu_ref[0])
    def _():
        x = jnp.concatenate([x_ref[:, s, :] for s in range(ROW_TILES)], axis=-1)
        hgu = jnp.dot(x.astype(BF16), wgu_ref[0], preferred_element_type=F32) + bgu_ref[0]
        ngrp = 2 * D_FF // GU_GROUP
        g = jnp.concatenate([hgu[:, m * GU_GROUP:m * GU_GROUP + LANES] for m in range(ngrp)], axis=-1)
        up = jnp.concatenate([hgu[:, m * GU_GROUP + LANES:(m + 1) * GU_GROUP] for m in range(ngrp)],
                             axis=-1)
        g = jnp.minimum(g, SWIGLU_LIMIT)
        up = jnp.clip(up, -SWIGLU_LIMIT, SWIGLU_LIMIT)
        act = (up + 1.0) * (g * jax.nn.sigmoid(g * SWIGLU_ALPHA))
        y = jnp.dot(act.astype(BF16), wd_ref[0], preferred_element_type=F32) + bd_ref[0]
        for s in range(ROW_TILES):
            y_ref[:, s, :] = y[:, s * LANES:(s + 1) * LANES]


def _ffn(block_e, n_used, xg, wgu, bgu, wd, bd):
    nb = xg.shape[0] // EXPERT_BLOCK
    blk = lambda i, be, nu: (jnp.minimum(i, nu[0] - 1), 0, 0)
    wsel = lambda i, be, nu: (be[jnp.minimum(i, nu[0] - 1)], 0, 0)
    grid_spec = pltpu.PrefetchScalarGridSpec(
        num_scalar_prefetch=2,
        grid=(nb,),
        in_specs=[
            pl.BlockSpec((EXPERT_BLOCK, ROW_TILES, LANES), blk),
            pl.BlockSpec((1, D_MODEL, 2 * D_FF), wsel),
            pl.BlockSpec((1, 1, 2 * D_FF), wsel),
            pl.BlockSpec((1, D_FF, D_MODEL), wsel),
            pl.BlockSpec((1, 1, D_MODEL), wsel),
        ],
        out_specs=pl.BlockSpec((EXPERT_BLOCK, ROW_TILES, LANES), blk),
    )
    return pl.pallas_call(
        _ffn_kernel,
        out_shape=jax.ShapeDtypeStruct(xg.shape, F32),
        grid_spec=grid_spec,
        compiler_params=pltpu.CompilerParams(
            dimension_semantics=("arbitrary",), vmem_limit_bytes=VMEM_LIMIT),
        name="ffn",
    )(block_e, n_used, xg, wgu, bgu, wd, bd)


def _combine_kernel(dest_hbm, yb_hbm, gate_ref, h1_ref, g2_ref, b2_ref, out_ref,
                    idx_smem, gbuf, isem, sem):
    tm = h1_ref.shape[0]
    i = pl.program_id(0)
    cp = pltpu.make_async_copy(dest_hbm.at[i], idx_smem, isem)
    cp.start()
    cp.wait()
    body = _row_copies(lambda j, k, r: yb_hbm.at[r], lambda j, k, r: gbuf.at[k, j],
                       idx_smem, sem, tm)

    def start(j, c):
        body(j, lambda d: d.start())
        return c

    def wait(j, c):
        body(j, lambda d: d.wait())
        return c

    lax.fori_loop(0, tm, start, 0)
    lax.fori_loop(0, tm, wait, 0)

    acc = ALPHA * h1_ref[...]
    for k in range(TOP_K):
        yk = jnp.concatenate([gbuf[k, :, s, :] for s in range(ROW_TILES)], axis=-1)
        acc = acc + gate_ref[:, k:k + 1] * yk
    out_ref[...] = _ln(acc, g2_ref[...], b2_ref[...])


def _combine(dest_tiles, ybuf, gate_t, h1, g2, b2, tm):
    n = h1.shape[0]
    return pl.pallas_call(
        _combine_kernel,
        out_shape=jax.ShapeDtypeStruct((n, D_MODEL), F32),
        grid=(n // tm,),
        in_specs=[
            pl.BlockSpec(memory_space=pl.ANY),
            pl.BlockSpec(memory_space=pl.ANY),
            pl.BlockSpec((tm, TOP_K), lambda i: (i, 0)),
            pl.BlockSpec((tm, D_MODEL), lambda i: (i, 0)),
            _const_spec((1, D_MODEL)), _const_spec((1, D_MODEL)),
        ],
        out_specs=pl.BlockSpec((tm, D_MODEL), lambda i: (i, 0)),
        scratch_shapes=[pltpu.SMEM((TOP_K * tm,), jnp.int32),
                        pltpu.VMEM((TOP_K, tm, ROW_TILES, LANES), F32),
                        pltpu.SemaphoreType.DMA, pltpu.SemaphoreType.DMA],
        compiler_params=pltpu.CompilerParams(
            dimension_semantics=("arbitrary",), vmem_limit_bytes=VMEM_LIMIT),
        name="combine",
    )(dest_tiles, ybuf, gate_t, h1, g2, b2)


def _pick_tile(t, pref):
    return pref if t % pref == 0 else t


def kernel(x_prompt, x_sample, cache_meta_k, cache_meta_v, cache_win_k, cache_win_v, state_conv,
           meta_tokens, ln_in_g, ln_in_b, w_in, b_in, conv_w, attn_sinks, w_attn_br, w_conv_br, w_o,
           ln1_g, ln1_b, w_router, b_router, w_gu, b_gu, w_d, b_d, ln2_g, ln2_b):
    bp, tp, _ = x_prompt.shape
    bs, ts, _ = x_sample.shape
    row2 = lambda a: a.reshape(1, -1)

    w_in_b = w_in[0].astype(BF16)
    b_in_r = row2(b_in[0])
    gin, bin_ = row2(ln_in_g), row2(ln_in_b)
    wa, wc, wo = w_attn_br[0].astype(BF16), w_conv_br[0].astype(BF16), w_o[0].astype(BF16)
    wr_t = w_router[0].T.astype(BF16)
    br = b_router[0].reshape(N_EXPERTS, 1)
    wgu = _wprep(w_gu[0])
    bgu = (b_gu[0].reshape(N_EXPERTS, 2 * D_FF // GU_GROUP, LANES, 2).transpose(0, 1, 3, 2)
           .reshape(N_EXPERTS, 1, 2 * D_FF))
    wd = w_d[0].astype(BF16)
    bd = b_d[0][:, None, :]
    sinks = attn_sinks[0]

    zbuf = jnp.zeros((1, CONV_W - 1, D_CONV), F32)
    _, _, _, _, km, vm, um_last = _inproj(meta_tokens[None], zbuf, w_in_b, b_in_r, conv_w[0],
                                          gin, bin_, N_META)

    def stream(x, cbuf, past_k, past_v, mk, mv, past_valid, tm_pref):
        bsz, t, _ = x.shape
        q, ga, gc, oc, k, v, u_last = _inproj(x, cbuf, w_in_b, b_in_r, conv_w[0], gin, bin_,
                                             _pick_tile(t, tm_pref))
        tpad = -(-t // CHUNK) * CHUNK
        padq = ((0, 0), (0, tpad - t), (0, 0))
        kfull = jnp.concatenate([past_k, jnp.pad(k, padq)], axis=1).astype(BF16)
        vfull = jnp.concatenate([past_v, jnp.pad(v, padq)], axis=1).astype(BF16)
        oa = _attention(jnp.pad(q, padq), kfull, vfull, mk.astype(BF16), mv.astype(BF16), sinks,
                        past_valid, t, _pick_tile(tpad, tm_pref))[:, :t]
        n = bsz * t
        flat = lambda a: a.reshape(n, a.shape[-1])
        h1, idx, gate, cnt = _mix(flat(x), flat(oa), flat(oc), flat(ga), flat(gc), wa, wc, wo,
                                  gin, bin_, row2(ln1_g[0]), row2(ln1_b[0]), wr_t, br,
                                  _pick_tile(n, tm_pref))
        return h1, idx, gate, cnt, k, v, u_last

    zk = jnp.zeros((bp, WINDOW, KV_W), F32)
    h1p, idxp, gatep, cntp, kp, vp, ulp = stream(x_prompt, um_last, zk, zk, km, vm, False, 512)
    wk_s = cache_win_k[0].reshape(bs, WINDOW, KV_W)
    wv_s = cache_win_v[0].reshape(bs, WINDOW, KV_W)
    h1s, idxs, gates, cnts, ks, vs, uls = stream(
        x_sample, state_conv[0], wk_s, wv_s, cache_meta_k[0].reshape(bs, N_META, KV_W),
        cache_meta_v[0].reshape(bs, N_META, KV_W), True, 512)

    h1 = jnp.concatenate([h1p, h1s], axis=0)
    idx = jnp.concatenate([idxp, idxs], axis=1)
    gate = jnp.concatenate([gatep, gates], axis=1)
    n = h1.shape[0]
    tmm = EXPERT_BLOCK
    counts = (cntp[:, 0] + cnts[:, 0]).astype(jnp.int32)
    padded = (counts + EXPERT_BLOCK - 1) // EXPERT_BLOCK * EXPERT_BLOCK
    pend = jnp.cumsum(padded)
    poff = pend - padded
    nb = -(-(n * TOP_K) // EXPERT_BLOCK) + N_EXPERTS
    block_e = jnp.minimum(
        jnp.sum((pend[None, :] <= (jnp.arange(nb) * EXPERT_BLOCK)[:, None]).astype(jnp.int32), axis=1),
        N_EXPERTS - 1).astype(jnp.int32)
    n_used = (pend[-1:] // EXPERT_BLOCK).astype(jnp.int32)
    poff_b = jnp.broadcast_to(poff.astype(F32)[:, None], (N_EXPERTS, LANES))

    dest = _plan(idx, poff_b, tmm)
    dest_tiles = dest.reshape(TOP_K, n // tmm, tmm).transpose(1, 0, 2).reshape(n // tmm, TOP_K * tmm)
    h3 = h1.reshape(n, ROW_TILES, LANES)
    xg = _dispatch(dest_tiles, h3, jnp.zeros((nb * EXPERT_BLOCK, ROW_TILES, LANES), F32), tmm)
    ybuf = _ffn(block_e, n_used, xg, wgu, bgu, wd, bd)
    y = _combine(dest_tiles, ybuf, gate.T, h1, row2(ln2_g[0]), row2(ln2_b[0]), tmm)

    y_prompt = y[:bp * tp].reshape(bp, tp, D_MODEL)
    y_sample = y[bp * tp:].reshape(bs, ts, D_MODEL)
    kv5 = lambda a: a.reshape(a.shape[0], a.shape[1], N_KV, HEAD_DIM)[None]
    mk_p = jnp.broadcast_to(kv5(km), (1, bp, N_META, N_KV, HEAD_DIM))
    mv_p = jnp.broadcast_to(kv5(vm), (1, bp, N_META, N_KV, HEAD_DIM))
    wk_p = kv5(kp[:, -WINDOW:])
    wv_p = kv5(vp[:, -WINDOW:])
    wk_o = kv5(jnp.concatenate([wk_s, ks], axis=1)[:, -WINDOW:])
    wv_o = kv5(jnp.concatenate([wv_s, vs], axis=1)[:, -WINDOW:])
    return (y_prompt, y_sample, mk_p, mv_p, wk_p, wv_p, ulp[None], wk_o, wv_o, uls[None])
```

```python
import functools

import jax
import jax.numpy as jnp
from jax import lax
from jax.experimental import pallas as pl
from jax.experimental.pallas import tpu as pltpu

D_MODEL = 1024
CHUNK = 64
N_META = 16
N_Q = 16
N_KV = 4
GROUP = N_Q // N_KV
HEAD_DIM = 64
ATTN_W = N_Q * HEAD_DIM
KV_W = N_KV * HEAD_DIM
WINDOW = 128
D_CONV = D_MODEL
CONV_W = 3
N_EXPERTS = 32
TOP_K = 4
D_FF = D_MODEL
SWIGLU_LIMIT = 7.0
SWIGLU_ALPHA = 1.702
EXPERT_BLOCK = 256
LN_EPS = 1e-5
DEPTH = 1
ALPHA = (2 * DEPTH) ** 0.25
ATTN_SCALE = HEAD_DIM ** -0.5

OFF_GA = ATTN_W
OFF_GC = OFF_GA + D_MODEL
OFF_CB = OFF_GC + D_MODEL
OFF_K = OFF_CB + D_CONV
OFF_V = OFF_K + KV_W
OFF_CC = OFF_V + KV_W
OFF_CH = OFF_CC + D_CONV
IN_COLS = OFF_CH + D_CONV

LANES = 128
VMEM_LIMIT = 56 * 1024 * 1024

F32 = jnp.float32
BF16 = jnp.bfloat16


def _ln(x, g, b):
    mu = jnp.mean(x, axis=-1, keepdims=True)
    xc = x - mu
    var = jnp.mean(xc * xc, axis=-1, keepdims=True)
    return (xc * lax.rsqrt(var + LN_EPS)) * g + b


def _const_spec(shape):
    nd = len(shape)
    return pl.BlockSpec(shape, lambda *_: (0,) * nd, pipeline_mode=pl.Buffered(1))


def _inproj_kernel(x_ref, cbuf_ref, w_ref, b_ref, cw_ref, g_ref, be_ref,
                   q_ref, ga_ref, gc_ref, oc_ref, k_ref, v_ref, ul_ref, carry_ref):
    tm = x_ref.shape[1]

    @pl.when(pl.program_id(1) == 0)
    def _():
        carry_ref[...] = cbuf_ref[0]

    h = _ln(x_ref[0], g_ref[...], be_ref[...]).astype(BF16)

    def proj(off, width):
        return (jnp.dot(h, w_ref[:, off:off + width], preferred_element_type=F32)
                + b_ref[:, off:off + width])

    q_ref[0] = (proj(0, ATTN_W) * ATTN_SCALE).astype(BF16)
    ga_ref[0] = jax.nn.sigmoid(proj(OFF_GA, D_MODEL)).astype(BF16)
    gc_ref[0] = jax.nn.sigmoid(proj(OFF_GC, D_MODEL)).astype(BF16)
    k_ref[0] = proj(OFF_K, KV_W)
    v_ref[0] = proj(OFF_V, KV_W)

    u = proj(OFF_CC, D_CONV) * proj(OFF_CH, D_CONV)
    prev = carry_ref[...]
    row = lax.broadcasted_iota(jnp.int32, (tm, D_CONV), 0)
    u1 = jnp.where(row == 0, prev[1:2], pltpu.roll(u, 1, 0))
    u2 = jnp.where(row == 0, prev[0:1],
                   jnp.where(row == 1, prev[1:2], pltpu.roll(u, 2, 0)))
    conv = cw_ref[0:1] * u2 + cw_ref[1:2] * u1 + cw_ref[2:3] * u
    oc_ref[0] = (proj(OFF_CB, D_CONV) * conv).astype(BF16)
    last = u[tm - (CONV_W - 1):tm]
    carry_ref[...] = last
    ul_ref[0] = last


def _inproj(x, cbuf, w_in, b_in, conv_w, ln_g, ln_b, tm):
    bsz, t, _ = x.shape
    cb_map = (lambda b, i: (b, 0, 0)) if cbuf.shape[0] == bsz else (lambda b, i: (0, 0, 0))
    row_spec = lambda w: pl.BlockSpec((1, tm, w), lambda b, i: (b, i, 0))
    outs = (
        jax.ShapeDtypeStruct((bsz, t, ATTN_W), BF16),
        jax.ShapeDtypeStruct((bsz, t, D_MODEL), BF16),
        jax.ShapeDtypeStruct((bsz, t, D_MODEL), BF16),
        jax.ShapeDtypeStruct((bsz, t, D_CONV), BF16),
        jax.ShapeDtypeStruct((bsz, t, KV_W), F32),
        jax.ShapeDtypeStruct((bsz, t, KV_W), F32),
        jax.ShapeDtypeStruct((bsz, CONV_W - 1, D_CONV), F32),
    )
    return pl.pallas_call(
        _inproj_kernel,
        out_shape=outs,
        grid=(bsz, t // tm),
        in_specs=[
            row_spec(D_MODEL),
            pl.BlockSpec((1, CONV_W - 1, D_CONV), cb_map),
            _const_spec((D_MODEL, IN_COLS)),
            _const_spec((1, IN_COLS)),
            _const_spec((CONV_W, D_CONV)),
            _const_spec((1, D_MODEL)),
            _const_spec((1, D_MODEL)),
        ],
        out_specs=(
            row_spec(ATTN_W), row_spec(D_MODEL), row_spec(D_MODEL), row_spec(D_CONV),
            row_spec(KV_W), row_spec(KV_W),
            pl.BlockSpec((1, CONV_W - 1, D_CONV), lambda b, i: (b, 0, 0)),
        ),
        scratch_shapes=[pltpu.VMEM((CONV_W - 1, D_CONV), F32)],
        compiler_params=pltpu.CompilerParams(
            dimension_semantics=("arbitrary", "arbitrary"), vmem_limit_bytes=VMEM_LIMIT),
        name="inproj",
    )(x, cbuf, w_in, b_in, conv_w, ln_g, ln_b)


def _attn_kernel(sink_ref, q_ref, k_ref, v_ref, mk_ref, mv_ref, o_ref, *, past_valid, t_valid):
    tq = q_ref.shape[1]
    nchunk = tq // CHUNK
    nkeys = WINDOW + CHUNK
    unroll = 2 if nchunk % 2 == 0 else 1
    tile = pl.program_id(1)
    mk = mk_ref[0]
    mv = mv_ref[0]
    contract = (((1,), (1,)), ((), ()))
    qrow = lax.broadcasted_iota(jnp.int32, (GROUP * CHUNK, 1), 0)
    col = lax.broadcasted_iota(jnp.int32, (1, N_META + nkeys), 1)

    def one_chunk(ci):
        row0 = pl.multiple_of((tile * nchunk + ci) * CHUNK, CHUNK)
        qoff = pl.multiple_of(ci * CHUNK, CHUNK)
        kc = jnp.concatenate([mk, k_ref[0, pl.ds(row0, nkeys), :]], axis=0)
        vc = jnp.concatenate([mv, v_ref[0, pl.ds(row0, nkeys), :]], axis=0)
        qc = q_ref[0, pl.ds(qoff, CHUNK), :]
        kpos = row0 + col - N_META
        valid = kpos < WINDOW + t_valid
        if not past_valid:
            valid = jnp.logical_and(valid, kpos >= WINDOW)
        valid = jnp.logical_or(valid, col < N_META)
        for g in range(N_KV):
            heads = [qc[:, (g * GROUP + i) * HEAD_DIM:(g * GROUP + i + 1) * HEAD_DIM]
                     for i in range(GROUP)]
            qg = jnp.concatenate(heads, axis=0)
            ksl = slice(g * HEAD_DIM, (g + 1) * HEAD_DIM)
            s = lax.dot_general(qg, kc[:, ksl], contract, preferred_element_type=F32)
            s = jnp.where(valid, s, -jnp.inf)
            sink = jnp.full((GROUP * CHUNK, 1), sink_ref[g * GROUP], F32)
            for i in range(1, GROUP):
                sink = jnp.where(qrow >= i * CHUNK, sink_ref[g * GROUP + i], sink)
            m = jnp.maximum(jnp.max(s, axis=-1, keepdims=True), sink)
            p = jnp.exp(s - m)
            den = jnp.sum(p, axis=-1, keepdims=True) + jnp.exp(sink - m)
            o = jnp.dot((p / den).astype(BF16), vc[:, ksl], preferred_element_type=F32)
            for i in range(GROUP):
                h = g * GROUP + i
                o_ref[0, pl.ds(qoff, CHUNK),
                      h * HEAD_DIM:(h + 1) * HEAD_DIM] = o[i * CHUNK:(i + 1) * CHUNK].astype(BF16)

    def body(it, carry):
        for u in range(unroll):
            one_chunk(it * unroll + u)
        return carry

    lax.fori_loop(0, nchunk // unroll, body, 0)


def _attention(q, kfull, vfull, mk, mv, sinks, past_valid, t_valid, tq):
    bsz, t, _ = q.shape
    tk = kfull.shape[1]
    m_map = (lambda b, i: (b, 0, 0)) if mk.shape[0] == bsz else (lambda b, i: (0, 0, 0))
    return pl.pallas_call(
        functools.partial(_attn_kernel, past_valid=past_valid, t_valid=t_valid),
        out_shape=jax.ShapeDtypeStruct((bsz, t, ATTN_W), BF16),
        grid=(bsz, t // tq),
        in_specs=[
            pl.BlockSpec(memory_space=pltpu.SMEM),
            pl.BlockSpec((1, tq, ATTN_W), lambda b, i: (b, i, 0)),
            pl.BlockSpec((1, tk, KV_W), lambda b, i: (b, 0, 0)),
            pl.BlockSpec((1, tk, KV_W), lambda b, i: (b, 0, 0)),
            pl.BlockSpec((1, N_META, KV_W), m_map),
            pl.BlockSpec((1, N_META, KV_W), m_map),
        ],
        out_specs=pl.BlockSpec((1, tq, ATTN_W), lambda b, i: (b, i, 0)),
        compiler_params=pltpu.CompilerParams(
            dimension_semantics=("arbitrary", "arbitrary"), vmem_limit_bytes=VMEM_LIMIT),
        name="attn",
    )(sinks, q, kfull, vfull, mk, mv)


def _mix_kernel(x_ref, oa_ref, oc_ref, ga_ref, gc_ref, wa_ref, wc_ref, wo_ref,
                gin_ref, bin_ref, g1_ref, b1_ref, wr_ref, br_ref,
                h1_ref, idx_ref, gate_ref, cnt_ref):
    tm = x_ref.shape[0]

    @pl.when(pl.program_id(0) == 0)
    def _():
        cnt_ref[...] = jnp.zeros_like(cnt_ref)

    h0 = _ln(x_ref[...], gin_ref[...], bin_ref[...])
    a = jnp.dot(oa_ref[...], wa_ref[...], preferred_element_type=F32)
    c = jnp.dot(oc_ref[...], wc_ref[...], preferred_element_type=F32)
    mixed = ga_ref[...].astype(F32) * a + gc_ref[...].astype(F32) * c
    y = jnp.dot(mixed.astype(BF16), wo_ref[...], preferred_element_type=F32)
    h1 = _ln(ALPHA * h0 + y, g1_ref[...], b1_ref[...])
    h1_ref[...] = h1

    logits = lax.dot_general(wr_ref[...], h1.astype(BF16), (((1,), (1,)), ((), ())),
                             preferred_element_type=F32) + br_ref[...]
    e_iota = lax.broadcasted_iota(jnp.int32, (N_EXPERTS, tm), 0)
    vals, idxs = [], []
    onehot = jnp.zeros((N_EXPERTS, tm), F32)
    for _ in range(TOP_K):
        mx = jnp.max(logits, axis=0, keepdims=True)
        ix = jnp.min(jnp.where(logits == mx, e_iota, N_EXPERTS), axis=0, keepdims=True)
        sel = e_iota == ix
        vals.append(mx)
        idxs.append(ix)
        onehot = onehot + sel.astype(F32)
        logits = jnp.where(sel, -jnp.inf, logits)
    ev = [jnp.exp(v - vals[0]) for v in vals]
    den = ev[0] + ev[1] + ev[2] + ev[3]
    idx_ref[...] = jnp.concatenate(idxs, axis=0)
    gate_ref[...] = jnp.concatenate([e / den for e in ev], axis=0)
    cnt_ref[...] += jnp.sum(onehot, axis=1, keepdims=True)


def _mix(x, oa, oc, ga, gc, wa, wc, wo, gin, bin_, g1, b1, wr_t, br, tm, n_total, row_off, prev):
    n = x.shape[0]
    off = row_off // tm
    outs = (
        jax.ShapeDtypeStruct((n_total, D_MODEL), F32),
        jax.ShapeDtypeStruct((TOP_K, n_total), jnp.int32),
        jax.ShapeDtypeStruct((TOP_K, n_total), F32),
        jax.ShapeDtypeStruct((N_EXPERTS, LANES), F32),
    )
    nprev = 0 if prev is None else len(prev)
    ntile = n // tm
    extra = 0 if prev is not None else -(-(n_total - n) // tm)
    row = lambda w: pl.BlockSpec((tm, w), lambda i: (jnp.minimum(i, ntile - 1), 0))

    def kern(*refs):
        refs = refs[nprev:]
        if extra == 0:
            _mix_kernel(*refs)
            return

        @pl.when(pl.program_id(0) < ntile)
        def _():
            _mix_kernel(*refs)

        @pl.when(pl.program_id(0) >= ntile)
        def _():
            for o_ref in refs[-4:-1]:
                o_ref[...] = jnp.zeros_like(o_ref)

    return pl.pallas_call(
        kern,
        out_shape=outs,
        grid=(ntile + extra,),
        in_specs=[pl.BlockSpec(memory_space=pl.ANY)] * nprev + [
            row(D_MODEL), row(ATTN_W), row(D_CONV), row(D_MODEL), row(D_MODEL),
            _const_spec((ATTN_W, D_MODEL)), _const_spec((D_CONV, D_MODEL)),
            _const_spec((D_MODEL, D_MODEL)),
            _const_spec((1, D_MODEL)), _const_spec((1, D_MODEL)),
            _const_spec((1, D_MODEL)), _const_spec((1, D_MODEL)),
            _const_spec((N_EXPERTS, D_MODEL)), _const_spec((N_EXPERTS, 1)),
        ],
        out_specs=(
            pl.BlockSpec((tm, D_MODEL), lambda i: (i + off, 0)),
            pl.BlockSpec((TOP_K, tm), lambda i: (0, i + off)),
            pl.BlockSpec((TOP_K, tm), lambda i: (0, i + off)),
            pl.BlockSpec((N_EXPERTS, LANES), lambda i: (0, 0)),
        ),
        input_output_aliases={j: j for j in range(nprev)},
        compiler_params=pltpu.CompilerParams(
            dimension_semantics=("arbitrary",), vmem_limit_bytes=VMEM_LIMIT),
        name="mix",
    )(*(prev or ()), x, oa, oc, ga, gc, wa, wc, wo, gin, bin_, g1, b1, wr_t, br)


def _plan_kernel(idx_ref, poff_ref, dest_ref, run_ref):
    tm = idx_ref.shape[1]

    @pl.when(pl.program_id(0) == 0)
    def _():
        run_ref[...] = jnp.zeros_like(run_ref)

    e_iota = lax.broadcasted_iota(jnp.int32, (N_EXPERTS, tm), 0)
    sels = [e_iota == idx_ref[k:k + 1, :] for k in range(TOP_K)]
    oh = sels[0].astype(F32)
    for k in range(1, TOP_K):
        oh = oh + sels[k].astype(F32)
    tri = (lax.broadcasted_iota(jnp.int32, (tm, tm), 0)
           < lax.broadcasted_iota(jnp.int32, (tm, tm), 1)).astype(BF16)
    excl = jnp.dot(oh.astype(BF16), tri, preferred_element_type=F32)
    pos = excl + (run_ref[:, 0:1] + poff_ref[:, 0:1])
    dest = [jnp.sum(jnp.where(s, pos, 0.0), axis=0, keepdims=True) for s in sels]
    dest_ref[...] = jnp.concatenate(dest, axis=0).astype(jnp.int32)
    run_ref[...] += jnp.sum(oh, axis=1, keepdims=True)


def _plan(idx, poff, tm):
    n = idx.shape[1]
    return pl.pallas_call(
        _plan_kernel,
        out_shape=jax.ShapeDtypeStruct((TOP_K, n), jnp.int32),
        grid=(n // tm,),
        in_specs=[pl.BlockSpec((TOP_K, tm), lambda i: (0, i)),
                  _const_spec((N_EXPERTS, LANES))],
        out_specs=pl.BlockSpec((TOP_K, tm), lambda i: (0, i)),
        scratch_shapes=[pltpu.VMEM((N_EXPERTS, LANES), F32)],
        compiler_params=pltpu.CompilerParams(dimension_semantics=("arbitrary",)),
        name="plan",
    )(idx, poff)


def _row_copies(src_of, dst_of, idx_smem, sem, tm, base=0):
    def body(j, fn):
        for k in range(TOP_K):
            r = idx_smem[base + k * tm + j]
            fn(pltpu.make_async_copy(src_of(j, k, r), dst_of(j, k, r), sem), k)
    return body


def _dispatch_kernel(last_ref, dest_hbm, h_ref, xg_hbm, idx_smem, zero_ref, isem, sem):
    tm = h_ref.shape[0]
    i = pl.program_id(0)

    @pl.when(i == 0)
    def _():
        zero_ref[...] = jnp.zeros_like(zero_ref)
        for e in range(N_EXPERTS):
            @pl.when(last_ref[e] >= 0)
            def _():
                blk = pl.multiple_of(last_ref[e], EXPERT_BLOCK)
                z = pltpu.make_async_copy(zero_ref, xg_hbm.at[pl.ds(blk, EXPERT_BLOCK), :], isem)
                z.start()
                z.wait()

    cp = pltpu.make_async_copy(dest_hbm.at[i], idx_smem, isem)
    cp.start()
    cp.wait()
    body = _row_copies(lambda j, k, r: h_ref.at[pl.ds(j, 1), :],
                       lambda j, k, r: xg_hbm.at[pl.ds(r, 1), :], idx_smem, sem, tm)

    def start(j, c):
        body(j, lambda d, k: d.start(priority=k % 2))
        return c

    def wait(j, c):
        body(j, lambda d, k: d.wait())
        return c

    lax.fori_loop(0, tm, start, 0)
    lax.fori_loop(0, tm, wait, 0)


def _dispatch(last_blk, dest_tiles, h1, nrows, tm):
    ntile = dest_tiles.shape[0]
    grid_spec = pltpu.PrefetchScalarGridSpec(
        num_scalar_prefetch=1,
        grid=(ntile,),
        in_specs=[pl.BlockSpec(memory_space=pl.ANY),
                  pl.BlockSpec((tm, D_MODEL), lambda i, lb: (i, 0))],
        out_specs=pl.BlockSpec(memory_space=pl.ANY),
        scratch_shapes=[pltpu.SMEM((TOP_K * tm,), jnp.int32),
                        pltpu.VMEM((EXPERT_BLOCK, D_MODEL), F32),
                        pltpu.SemaphoreType.DMA, pltpu.SemaphoreType.DMA],
    )
    return pl.pallas_call(
        _dispatch_kernel,
        out_shape=jax.ShapeDtypeStruct((nrows, D_MODEL), F32),
        grid_spec=grid_spec,
        compiler_params=pltpu.CompilerParams(dimension_semantics=("arbitrary",)),
        name="dispatch",
    )(last_blk, dest_tiles, h1)


GU_GROUP = 2 * LANES


def _wprep_kernel(w_ref, o_ref):
    r = lax.broadcasted_iota(jnp.int32, (GU_GROUP, GU_GROUP), 0)
    c = lax.broadcasted_iota(jnp.int32, (GU_GROUP, GU_GROUP), 1)
    perm = (r == jnp.where(c < LANES, 2 * c, 2 * (c - LANES) + 1)).astype(BF16)
    for m in range(2 * D_FF // GU_GROUP):
        cols = slice(m * GU_GROUP, (m + 1) * GU_GROUP)
        o_ref[0, :, cols] = jnp.dot(w_ref[0, :, cols].astype(BF16), perm,
                                    preferred_element_type=F32).astype(BF16)


def _wprep(w_gu):
    ne = w_gu.shape[0]
    return pl.pallas_call(
        _wprep_kernel,
        out_shape=jax.ShapeDtypeStruct(w_gu.shape, BF16),
        grid=(ne,),
        in_specs=[pl.BlockSpec((1, D_MODEL, 2 * D_FF), lambda e: (e, 0, 0))],
        out_specs=pl.BlockSpec((1, D_MODEL, 2 * D_FF), lambda e: (e, 0, 0)),
        compiler_params=pltpu.CompilerParams(
            dimension_semantics=("arbitrary",), vmem_limit_bytes=VMEM_LIMIT),
        name="wprep",
    )(w_gu)


def _ffn_kernel(be_ref, nu_ref, x_ref, wgu_ref, bgu_ref, wd_ref, bd_ref, y_ref):
    del be_ref

    @pl.when(pl.program_id(0) < nu_ref[0])
    def _():
        hgu = jnp.dot(x_ref[...].astype(BF16), wgu_ref[0], preferred_element_type=F32) + bgu_ref[0]
        ngrp = 2 * D_FF // GU_GROUP
        g = jnp.concatenate([hgu[:, m * GU_GROUP:m * GU_GROUP + LANES] for m in range(ngrp)], axis=-1)
        up = jnp.concatenate([hgu[:, m * GU_GROUP + LANES:(m + 1) * GU_GROUP] for m in range(ngrp)],
                             axis=-1)
        g = jnp.minimum(g, SWIGLU_LIMIT)
        up = jnp.clip(up, -SWIGLU_LIMIT, SWIGLU_LIMIT)
        act = (up + 1.0) * (g * jax.nn.sigmoid(g * SWIGLU_ALPHA))
        y_ref[...] = jnp.dot(act.astype(BF16), wd_ref[0], preferred_element_type=F32) + bd_ref[0]


def _ffn(block_e, n_used, xg, wgu, bgu, wd, bd):
    nb = xg.shape[0] // EXPERT_BLOCK
    blk = lambda i, be, nu: (jnp.minimum(i, nu[0] - 1), 0)
    wsel = lambda i, be, nu: (be[jnp.minimum(i, nu[0] - 1)], 0, 0)
    grid_spec = pltpu.PrefetchScalarGridSpec(
        num_scalar_prefetch=2,
        grid=(nb,),
        in_specs=[
            pl.BlockSpec((EXPERT_BLOCK, D_MODEL), blk),
            pl.BlockSpec((1, D_MODEL, 2 * D_FF), wsel),
            pl.BlockSpec((1, 1, 2 * D_FF), wsel),
            pl.BlockSpec((1, D_FF, D_MODEL), wsel),
            pl.BlockSpec((1, 1, D_MODEL), wsel),
        ],
        out_specs=pl.BlockSpec((EXPERT_BLOCK, D_MODEL), blk),
    )
    return pl.pallas_call(
        _ffn_kernel,
        out_shape=jax.ShapeDtypeStruct(xg.shape, F32),
        grid_spec=grid_spec,
        compiler_params=pltpu.CompilerParams(
            dimension_semantics=("arbitrary",), vmem_limit_bytes=VMEM_LIMIT),
        name="ffn",
    )(block_e, n_used, xg, wgu, bgu, wd, bd)


def _combine_kernel(dest_hbm, yb_hbm, gate_ref, h1_ref, g2_ref, b2_ref, outa_ref, outb_ref,
                    idx_smem, gbuf, isem, sem, *, tiles_a):
    tm = h1_ref.shape[0]
    i = pl.program_id(0)
    slot = i % 2

    def gather(t, s, start):
        if start:
            cp = pltpu.make_async_copy(dest_hbm.at[t], idx_smem.at[pl.ds(s * (TOP_K * tm), TOP_K * tm)],
                                       isem)
            cp.start()
            cp.wait()
        body = _row_copies(lambda j, k, r: yb_hbm.at[pl.ds(r, 1), :],
                           lambda j, k, r: gbuf.at[s, k, pl.ds(j, 1), :], idx_smem, sem.at[s], tm,
                           base=s * (TOP_K * tm))

        def step(j, c):
            body(j, (lambda d, k: d.start(priority=k % 2)) if start else (lambda d, k: d.wait()))
            return c

        lax.fori_loop(0, tm, step, 0)

    @pl.when(i == 0)
    def _():
        gather(0, 0, True)

    @pl.when(i + 1 < pl.num_programs(0))
    def _():
        gather(i + 1, 1 - slot, True)

    gather(i, slot, False)
    acc = ALPHA * h1_ref[...]
    for k in range(TOP_K):
        acc = acc + gate_ref[:, k:k + 1] * gbuf[slot, k]
    out = _ln(acc, g2_ref[...], b2_ref[...])

    @pl.when(i < tiles_a)
    def _():
        outa_ref[...] = out

    @pl.when(i >= tiles_a)
    def _():
        outb_ref[...] = out


def _combine(dest_tiles, ybuf, gate_t, h1, g2, b2, tm, n_a):
    n = h1.shape[0]
    tiles_a = n_a // tm
    return pl.pallas_call(
        functools.partial(_combine_kernel, tiles_a=tiles_a),
        out_shape=(jax.ShapeDtypeStruct((n_a, D_MODEL), F32),
                   jax.ShapeDtypeStruct((n - n_a, D_MODEL), F32)),
        grid=(n // tm,),
        in_specs=[
            pl.BlockSpec(memory_space=pl.ANY),
            pl.BlockSpec(memory_space=pl.ANY),
            pl.BlockSpec((tm, TOP_K), lambda i: (i, 0)),
            pl.BlockSpec((tm, D_MODEL), lambda i: (i, 0)),
            _const_spec((1, D_MODEL)), _const_spec((1, D_MODEL)),
        ],
        out_specs=(pl.BlockSpec((tm, D_MODEL), lambda i: (jnp.minimum(i, tiles_a - 1), 0)),
                   pl.BlockSpec((tm, D_MODEL), lambda i: (jnp.maximum(i - tiles_a, 0), 0))),
        scratch_shapes=[pltpu.SMEM((2 * TOP_K * tm,), jnp.int32),
                        pltpu.VMEM((2, TOP_K, tm, D_MODEL), F32),
                        pltpu.SemaphoreType.DMA, pltpu.SemaphoreType.DMA((2,))],
        compiler_params=pltpu.CompilerParams(
            dimension_semantics=("arbitrary",), vmem_limit_bytes=VMEM_LIMIT),
        name="combine",
    )(dest_tiles, ybuf, gate_t, h1, g2, b2)


def _pick_tile(t, pref):
    return pref if t % pref == 0 else t


def kernel(x_prompt, x_sample, cache_meta_k, cache_meta_v, cache_win_k, cache_win_v, state_conv,
           meta_tokens, ln_in_g, ln_in_b, w_in, b_in, conv_w, attn_sinks, w_attn_br, w_conv_br, w_o,
           ln1_g, ln1_b, w_router, b_router, w_gu, b_gu, w_d, b_d, ln2_g, ln2_b):
    bp, tp, _ = x_prompt.shape
    bs, ts, _ = x_sample.shape
    row2 = lambda a: a.reshape(1, -1)

    w_in_b = w_in[0].astype(BF16)
    b_in_r = row2(b_in[0])
    gin, bin_ = row2(ln_in_g), row2(ln_in_b)
    wa, wc, wo = w_attn_br[0].astype(BF16), w_conv_br[0].astype(BF16), w_o[0].astype(BF16)
    wr_t = w_router[0].T.astype(BF16)
    br = b_router[0].reshape(N_EXPERTS, 1)
    wgu = _wprep(w_gu[0])
    bgu = (b_gu[0].reshape(N_EXPERTS, 2 * D_FF // GU_GROUP, LANES, 2).transpose(0, 1, 3, 2)
           .reshape(N_EXPERTS, 1, 2 * D_FF))
    wd = w_d[0].astype(BF16)
    bd = b_d[0][:, None, :]
    sinks = attn_sinks[0]

    zbuf = jnp.zeros((1, CONV_W - 1, D_CONV), F32)
    _, _, _, _, km, vm, um_last = _inproj(meta_tokens[None], zbuf, w_in_b, b_in_r, conv_w[0],
                                          gin, bin_, N_META)

    n_total = bp * tp + bs * ts

    def stream(x, cbuf, past_k, past_v, mk, mv, past_valid, tm_pref, row_off, prev):
        bsz, t, _ = x.shape
        q, ga, gc, oc, k, v, u_last = _inproj(x, cbuf, w_in_b, b_in_r, conv_w[0], gin, bin_,
                                             _pick_tile(t, tm_pref))
        tpad = -(-t // CHUNK) * CHUNK
        padq = ((0, 0), (0, tpad - t), (0, 0))
        kfull = jnp.concatenate([past_k, jnp.pad(k, padq)], axis=1).astype(BF16)
        vfull = jnp.concatenate([past_v, jnp.pad(v, padq)], axis=1).astype(BF16)
        oa = _attention(jnp.pad(q, padq), kfull, vfull, mk.astype(BF16), mv.astype(BF16), sinks,
                        past_valid, t, _pick_tile(tpad, tm_pref))[:, :t]
        n = bsz * t
        flat = lambda a: a.reshape(n, a.shape[-1])
        h1, idx, gate, cnt = _mix(flat(x), flat(oa), flat(oc), flat(ga), flat(gc), wa, wc, wo,
                                  gin, bin_, row2(ln1_g[0]), row2(ln1_b[0]), wr_t, br,
                                  _pick_tile(n, tm_pref), n_total, row_off, prev)
        return h1, idx, gate, cnt, k, v, u_last

    zk = jnp.zeros((bp, WINDOW, KV_W), F32)
    h1p, idxp, gatep, cntp, kp, vp, ulp = stream(x_prompt, um_last, zk, zk, km, vm, False, 512, 0, None)
    wk_s = cache_win_k[0].reshape(bs, WINDOW, KV_W)
    wv_s = cache_win_v[0].reshape(bs, WINDOW, KV_W)
    h1, idx, gate, cnts, ks, vs, uls = stream(
        x_sample, state_conv[0], wk_s, wv_s, cache_meta_k[0].reshape(bs, N_META, KV_W),
        cache_meta_v[0].reshape(bs, N_META, KV_W), True, 512, bp * tp, (h1p, idxp, gatep))

    n = n_total
    tmm = EXPERT_BLOCK
    counts = (cntp[:, 0] + cnts[:, 0]).astype(jnp.int32)
    padded = (counts + EXPERT_BLOCK - 1) // EXPERT_BLOCK * EXPERT_BLOCK
    pend = jnp.cumsum(padded)
    poff = pend - padded
    nb = -(-(n * TOP_K) // EXPERT_BLOCK) + N_EXPERTS
    block_e = jnp.minimum(
        jnp.sum((pend[None, :] <= (jnp.arange(nb) * EXPERT_BLOCK)[:, None]).astype(jnp.int32), axis=1),
        N_EXPERTS - 1).astype(jnp.int32)
    n_used = (pend[-1:] // EXPERT_BLOCK).astype(jnp.int32)
    poff_b = jnp.broadcast_to(poff.astype(F32)[:, None], (N_EXPERTS, LANES))

    dest = _plan(idx, poff_b, tmm)
    dest_tiles = dest.reshape(TOP_K, n // tmm, tmm).transpose(1, 0, 2).reshape(n // tmm, TOP_K * tmm)
    last_blk = jnp.where(padded > 0, pend - EXPERT_BLOCK, -1).astype(jnp.int32)
    xg = _dispatch(last_blk, dest_tiles, h1, nb * EXPERT_BLOCK, tmm)
    ybuf = _ffn(block_e, n_used, xg, wgu, bgu, wd, bd)
    y_p, y_s = _combine(dest_tiles, ybuf, gate.T, h1, row2(ln2_g[0]), row2(ln2_b[0]), tmm, bp * tp)
    y_prompt = y_p.reshape(bp, tp, D_MODEL)
    y_sample = y_s.reshape(bs, ts, D_MODEL)
    kv5 = lambda a: a.reshape(a.shape[0], a.shape[1], N_KV, HEAD_DIM)[None]
    mk_p = jnp.broadcast_to(kv5(km), (1, bp, N_META, N_KV, HEAD_DIM))
    mv_p = jnp.broadcast_to(kv5(vm), (1, bp, N_META, N_KV, HEAD_DIM))
    wk_p = kv5(kp[:, -WINDOW:])
    wv_p = kv5(vp[:, -WINDOW:])
    wk_o = kv5(jnp.concatenate([wk_s, ks], axis=1)[:, -WINDOW:])
    wv_o = kv5(jnp.concatenate([wv_s, vs], axis=1)[:, -WINDOW:])
    return (y_prompt, y_sample, mk_p, mv_p, wk_p, wv_p, ulp[None], wk_o, wv_o, uls[None])
```

```python
import functools

import jax
import jax.numpy as jnp
from jax import lax
from jax.experimental import pallas as pl
from jax.experimental.pallas import tpu as pltpu

D_MODEL = 1024
CHUNK = 64
N_META = 16
N_Q = 16
N_KV = 4
GROUP = N_Q // N_KV
HEAD_DIM = 64
ATTN_W = N_Q * HEAD_DIM
KV_W = N_KV * HEAD_DIM
WINDOW = 128
D_CONV = D_MODEL
CONV_W = 3
N_EXPERTS = 32
TOP_K = 4
D_FF = D_MODEL
SWIGLU_LIMIT = 7.0
SWIGLU_ALPHA = 1.702
EXPERT_BLOCK = 256
LN_EPS = 1e-5
DEPTH = 1
ALPHA = (2 * DEPTH) ** 0.25
ATTN_SCALE = HEAD_DIM ** -0.5

OFF_GA = ATTN_W
OFF_GC = OFF_GA + D_MODEL
OFF_CB = OFF_GC + D_MODEL
OFF_K = OFF_CB + D_CONV
OFF_V = OFF_K + KV_W
OFF_CC = OFF_V + KV_W
OFF_CH = OFF_CC + D_CONV
IN_COLS = OFF_CH + D_CONV

LANES = 128
VMEM_LIMIT = 56 * 1024 * 1024

F32 = jnp.float32
BF16 = jnp.bfloat16


def _ln(x, g, b):
    mu = jnp.mean(x, axis=-1, keepdims=True)
    xc = x - mu
    var = jnp.mean(xc * xc, axis=-1, keepdims=True)
    return (xc * lax.rsqrt(var + LN_EPS)) * g + b


def _const_spec(shape):
    nd = len(shape)
    return pl.BlockSpec(shape, lambda *_: (0,) * nd, pipeline_mode=pl.Buffered(1))


def _inproj_kernel(x_ref, cbuf_ref, w_ref, b_ref, cw_ref, g_ref, be_ref,
                   q_ref, ga_ref, gc_ref, oc_ref, k_ref, v_ref, ul_ref, carry_ref):
    tm = x_ref.shape[1]

    @pl.when(pl.program_id(1) == 0)
    def _():
        carry_ref[...] = cbuf_ref[0]

    h = _ln(x_ref[0], g_ref[...], be_ref[...]).astype(BF16)

    def proj(off, width):
        return (jnp.dot(h, w_ref[:, off:off + width], preferred_element_type=F32)
                + b_ref[:, off:off + width])

    q_ref[0] = (proj(0, ATTN_W) * ATTN_SCALE).astype(BF16)
    ga_ref[0] = jax.nn.sigmoid(proj(OFF_GA, D_MODEL)).astype(BF16)
    gc_ref[0] = jax.nn.sigmoid(proj(OFF_GC, D_MODEL)).astype(BF16)
    k_ref[0] = proj(OFF_K, KV_W)
    v_ref[0] = proj(OFF_V, KV_W)

    u = proj(OFF_CC, D_CONV) * proj(OFF_CH, D_CONV)
    prev = carry_ref[...]
    row = lax.broadcasted_iota(jnp.int32, (tm, D_CONV), 0)
    u1 = jnp.where(row == 0, prev[1:2], pltpu.roll(u, 1, 0))
    u2 = jnp.where(row == 0, prev[0:1],
                   jnp.where(row == 1, prev[1:2], pltpu.roll(u, 2, 0)))
    conv = cw_ref[0:1] * u2 + cw_ref[1:2] * u1 + cw_ref[2:3] * u
    oc_ref[0] = (proj(OFF_CB, D_CONV) * conv).astype(BF16)
    last = u[tm - (CONV_W - 1):tm]
    carry_ref[...] = last
    ul_ref[0] = last


def _inproj(x, cbuf, w_in, b_in, conv_w, ln_g, ln_b, tm):
    bsz, t, _ = x.shape
    cb_map = (lambda b, i: (b, 0, 0)) if cbuf.shape[0] == bsz else (lambda b, i: (0, 0, 0))
    row_spec = lambda w: pl.BlockSpec((1, tm, w), lambda b, i: (b, i, 0))
    outs = (
        jax.ShapeDtypeStruct((bsz, t, ATTN_W), BF16),
        jax.ShapeDtypeStruct((bsz, t, D_MODEL), BF16),
        jax.ShapeDtypeStruct((bsz, t, D_MODEL), BF16),
        jax.ShapeDtypeStruct((bsz, t, D_CONV), BF16),
        jax.ShapeDtypeStruct((bsz, t, KV_W), F32),
        jax.ShapeDtypeStruct((bsz, t, KV_W), F32),
        jax.ShapeDtypeStruct((bsz, CONV_W - 1, D_CONV), F32),
    )
    return pl.pallas_call(
        _inproj_kernel,
        out_shape=outs,
        grid=(bsz, t // tm),
        in_specs=[
            row_spec(D_MODEL),
            pl.BlockSpec((1, CONV_W - 1, D_CONV), cb_map),
            _const_spec((D_MODEL, IN_COLS)),
            _const_spec((1, IN_COLS)),
            _const_spec((CONV_W, D_CONV)),
            _const_spec((1, D_MODEL)),
            _const_spec((1, D_MODEL)),
        ],
        out_specs=(
            row_spec(ATTN_W), row_spec(D_MODEL), row_spec(D_MODEL), row_spec(D_CONV),
            row_spec(KV_W), row_spec(KV_W),
            pl.BlockSpec((1, CONV_W - 1, D_CONV), lambda b, i: (b, 0, 0)),
        ),
        scratch_shapes=[pltpu.VMEM((CONV_W - 1, D_CONV), F32)],
        compiler_params=pltpu.CompilerParams(
            dimension_semantics=("arbitrary", "arbitrary"), vmem_limit_bytes=VMEM_LIMIT),
        name="inproj",
    )(x, cbuf, w_in, b_in, conv_w, ln_g, ln_b)


def _attn_kernel(sink_ref, q_ref, k_ref, v_ref, mk_ref, mv_ref, o_ref, *, past_valid, t_valid):
    tq = q_ref.shape[1]
    nchunk = tq // CHUNK
    nkeys = WINDOW + CHUNK
    unroll = 2 if nchunk % 2 == 0 else 1
    tile = pl.program_id(1)
    mk = mk_ref[0]
    mv = mv_ref[0]
    contract = (((1,), (1,)), ((), ()))
    qrow = lax.broadcasted_iota(jnp.int32, (GROUP * CHUNK, 1), 0)
    col = lax.broadcasted_iota(jnp.int32, (1, N_META + nkeys), 1)

    def one_chunk(ci):
        row0 = pl.multiple_of((tile * nchunk + ci) * CHUNK, CHUNK)
        qoff = pl.multiple_of(ci * CHUNK, CHUNK)
        kc = jnp.concatenate([mk, k_ref[0, pl.ds(row0, nkeys), :]], axis=0)
        vc = jnp.concatenate([mv, v_ref[0, pl.ds(row0, nkeys), :]], axis=0)
        qc = q_ref[0, pl.ds(qoff, CHUNK), :]
        kpos = row0 + col - N_META
        valid = kpos < WINDOW + t_valid
        if not past_valid:
            valid = jnp.logical_and(valid, kpos >= WINDOW)
        valid = jnp.logical_or(valid, col < N_META)
        for g in range(N_KV):
            heads = [qc[:, (g * GROUP + i) * HEAD_DIM:(g * GROUP + i + 1) * HEAD_DIM]
                     for i in range(GROUP)]
            qg = jnp.concatenate(heads, axis=0)
            ksl = slice(g * HEAD_DIM, (g + 1) * HEAD_DIM)
            s = lax.dot_general(qg, kc[:, ksl], contract, preferred_element_type=F32)
            s = jnp.where(valid, s, -jnp.inf)
            sink = jnp.full((GROUP * CHUNK, 1), sink_ref[g * GROUP], F32)
            for i in range(1, GROUP):
                sink = jnp.where(qrow >= i * CHUNK, sink_ref[g * GROUP + i], sink)
            m = jnp.maximum(jnp.max(s, axis=-1, keepdims=True), sink)
            p = jnp.exp(s - m)
            den = jnp.sum(p, axis=-1, keepdims=True) + jnp.exp(sink - m)
            o = jnp.dot((p / den).astype(BF16), vc[:, ksl], preferred_element_type=F32)
            for i in range(GROUP):
                h = g * GROUP + i
                o_ref[0, pl.ds(qoff, CHUNK),
                      h * HEAD_DIM:(h + 1) * HEAD_DIM] = o[i * CHUNK:(i + 1) * CHUNK].astype(BF16)

    def body(it, carry):
        for u in range(unroll):
            one_chunk(it * unroll + u)
        return carry

    lax.fori_loop(0, nchunk // unroll, body, 0)


def _attention(q, kfull, vfull, mk, mv, sinks, past_valid, t_valid, tq):
    bsz, t, _ = q.shape
    tk = kfull.shape[1]
    m_map = (lambda b, i: (b, 0, 0)) if mk.shape[0] == bsz else (lambda b, i: (0, 0, 0))
    return pl.pallas_call(
        functools.partial(_attn_kernel, past_valid=past_valid, t_valid=t_valid),
        out_shape=jax.ShapeDtypeStruct((bsz, t, ATTN_W), BF16),
        grid=(bsz, t // tq),
        in_specs=[
            pl.BlockSpec(memory_space=pltpu.SMEM),
            pl.BlockSpec((1, tq, ATTN_W), lambda b, i: (b, i, 0)),
            pl.BlockSpec((1, tk, KV_W), lambda b, i: (b, 0, 0)),
            pl.BlockSpec((1, tk, KV_W), lambda b, i: (b, 0, 0)),
            pl.BlockSpec((1, N_META, KV_W), m_map),
            pl.BlockSpec((1, N_META, KV_W), m_map),
        ],
        out_specs=pl.BlockSpec((1, tq, ATTN_W), lambda b, i: (b, i, 0)),
        compiler_params=pltpu.CompilerParams(
            dimension_semantics=("arbitrary", "arbitrary"), vmem_limit_bytes=VMEM_LIMIT),
        name="attn",
    )(sinks, q, kfull, vfull, mk, mv)


def _mix_kernel(x_ref, oa_ref, oc_ref, ga_ref, gc_ref, wa_ref, wc_ref, wo_ref,
                gin_ref, bin_ref, g1_ref, b1_ref, wr_ref, br_ref,
                h1_ref, idx_ref, gate_ref):
    tm = x_ref.shape[0]
    h0 = _ln(x_ref[...], gin_ref[...], bin_ref[...])
    a = jnp.dot(oa_ref[...], wa_ref[...], preferred_element_type=F32)
    c = jnp.dot(oc_ref[...], wc_ref[...], preferred_element_type=F32)
    mixed = ga_ref[...].astype(F32) * a + gc_ref[...].astype(F32) * c
    y = jnp.dot(mixed.astype(BF16), wo_ref[...], preferred_element_type=F32)
    h1 = _ln(ALPHA * h0 + y, g1_ref[...], b1_ref[...])
    h1_ref[...] = h1

    logits = lax.dot_general(wr_ref[...], h1.astype(BF16), (((1,), (1,)), ((), ())),
                             preferred_element_type=F32) + br_ref[...]
    e_iota = lax.broadcasted_iota(jnp.int32, (N_EXPERTS, tm), 0)
    vals, idxs = [], []
    for _ in range(TOP_K):
        mx = jnp.max(logits, axis=0, keepdims=True)
        ix = jnp.min(jnp.where(logits == mx, e_iota, N_EXPERTS), axis=0, keepdims=True)
        sel = e_iota == ix
        vals.append(mx)
        idxs.append(ix)
        logits = jnp.where(sel, -jnp.inf, logits)
    ev = [jnp.exp(v - vals[0]) for v in vals]
    den = ev[0] + ev[1] + ev[2] + ev[3]
    idx_ref[...] = jnp.concatenate(idxs, axis=0)
    gate_ref[...] = jnp.concatenate([e / den for e in ev], axis=0)


def _mix(x, oa, oc, ga, gc, wa, wc, wo, gin, bin_, g1, b1, wr_t, br, tm, n_total, row_off, prev):
    n = x.shape[0]
    off = row_off // tm
    outs = (
        jax.ShapeDtypeStruct((n_total, D_MODEL), F32),
        jax.ShapeDtypeStruct((TOP_K, n_total), jnp.int32),
        jax.ShapeDtypeStruct((TOP_K, n_total), F32),
    )
    nprev = 0 if prev is None else len(prev)
    ntile = n // tm
    extra = 0 if prev is not None else -(-(n_total - n) // tm)
    row = lambda w: pl.BlockSpec((tm, w), lambda i: (jnp.minimum(i, ntile - 1), 0))

    def kern(*refs):
        refs = refs[nprev:]
        if extra == 0:
            _mix_kernel(*refs)
            return

        @pl.when(pl.program_id(0) < ntile)
        def _():
            _mix_kernel(*refs)

        @pl.when(pl.program_id(0) >= ntile)
        def _():
            for o_ref in refs[-3:]:
                o_ref[...] = jnp.zeros_like(o_ref)

    return pl.pallas_call(
        kern,
        out_shape=outs,
        grid=(ntile + extra,),
        in_specs=[pl.BlockSpec(memory_space=pl.ANY)] * nprev + [
            row(D_MODEL), row(ATTN_W), row(D_CONV), row(D_MODEL), row(D_MODEL),
            _const_spec((ATTN_W, D_MODEL)), _const_spec((D_CONV, D_MODEL)),
            _const_spec((D_MODEL, D_MODEL)),
            _const_spec((1, D_MODEL)), _const_spec((1, D_MODEL)),
            _const_spec((1, D_MODEL)), _const_spec((1, D_MODEL)),
            _const_spec((N_EXPERTS, D_MODEL)), _const_spec((N_EXPERTS, 1)),
        ],
        out_specs=(
            pl.BlockSpec((tm, D_MODEL), lambda i: (i + off, 0)),
            pl.BlockSpec((TOP_K, tm), lambda i: (0, i + off)),
            pl.BlockSpec((TOP_K, tm), lambda i: (0, i + off)),
        ),
        input_output_aliases={j: j for j in range(nprev)},
        compiler_params=pltpu.CompilerParams(
            dimension_semantics=("arbitrary",), vmem_limit_bytes=VMEM_LIMIT),
        name="mix",
    )(*(prev or ()), x, oa, oc, ga, gc, wa, wc, wo, gin, bin_, g1, b1, wr_t, br)


SEG_ALIGN = 8
SEG_BITS = tuple(1 << b for b in range(8, 2, -1))


def _plan_kernel(idx_ref, lpos_ref, n8_ref):
    tm = idx_ref.shape[1]
    e_iota = lax.broadcasted_iota(jnp.int32, (N_EXPERTS, tm), 0)
    sels = [e_iota == idx_ref[k:k + 1, :] for k in range(TOP_K)]
    oh = sels[0].astype(F32)
    for k in range(1, TOP_K):
        oh = oh + sels[k].astype(F32)
    cnt = jnp.sum(oh, axis=1, keepdims=True)
    n8 = jnp.floor((cnt + (SEG_ALIGN - 1)) * (1.0 / SEG_ALIGN)) * SEG_ALIGN
    n8b = jnp.broadcast_to(n8, (N_EXPERTS, LANES))
    low = (lax.broadcasted_iota(jnp.int32, (N_EXPERTS, N_EXPERTS), 0)
           > lax.broadcasted_iota(jnp.int32, (N_EXPERTS, N_EXPERTS), 1)).astype(BF16)
    loff = jnp.dot(low, n8b.astype(BF16), preferred_element_type=F32)[:, 0:1]
    tri = (lax.broadcasted_iota(jnp.int32, (tm, tm), 0)
           < lax.broadcasted_iota(jnp.int32, (tm, tm), 1)).astype(BF16)
    pos = jnp.dot(oh.astype(BF16), tri, preferred_element_type=F32) + loff
    lpos = [jnp.sum(jnp.where(s, pos, 0.0), axis=0, keepdims=True) for s in sels]
    lpos_ref[...] = jnp.concatenate(lpos, axis=0).astype(jnp.int32)
    n8_ref[0] = n8b


def _plan(idx, tm):
    n = idx.shape[1]
    return pl.pallas_call(
        _plan_kernel,
        out_shape=(jax.ShapeDtypeStruct((TOP_K, n), jnp.int32),
                   jax.ShapeDtypeStruct((n // tm, N_EXPERTS, LANES), F32)),
        grid=(n // tm,),
        in_specs=[pl.BlockSpec((TOP_K, tm), lambda i: (0, i))],
        out_specs=(pl.BlockSpec((TOP_K, tm), lambda i: (0, i)),
                   pl.BlockSpec((1, N_EXPERTS, LANES), lambda i: (i, 0, 0))),
        compiler_params=pltpu.CompilerParams(dimension_semantics=("arbitrary",)),
        name="plan",
    )(idx)


def _segment_copies(n8_ref, lo_ref, gs_ref, t, local_of, global_of, sem, to_global):
    def run(action):
        for e in range(N_EXPERTS):
            n = n8_ref[t * N_EXPERTS + e]
            lo = lo_ref[t * N_EXPERTS + e]
            gs = gs_ref[t * N_EXPERTS + e]
            for bit in SEG_BITS:
                @pl.when((n & bit) != 0)
                def _():
                    off = n & (-2 * bit)
                    loc = local_of(pl.multiple_of(lo + off, SEG_ALIGN), bit)
                    glo = global_of(pl.multiple_of(gs + off, SEG_ALIGN), bit)
                    src, dst = (loc, glo) if to_global else (glo, loc)
                    action(pltpu.make_async_copy(src, dst, sem))
    return run


def _dispatch_kernel(last_ref, n8_ref, lo_ref, gs_ref, lpos_ref, h_ref, xg_hbm, xs_ref, zero_ref,
                     zsem, sem):
    tm = h_ref.shape[0]
    nloc = xs_ref.shape[1]
    i = pl.program_id(0)
    slot = i % 2

    @pl.when(i == 0)
    def _():
        zero_ref[...] = jnp.zeros_like(zero_ref)

        def zero_block(row):
            blk = pl.multiple_of(row, EXPERT_BLOCK)
            z = pltpu.make_async_copy(zero_ref, xg_hbm.at[pl.ds(blk, EXPERT_BLOCK), :], zsem)
            z.start()
            z.wait()

        for e in range(N_EXPERTS):
            @pl.when(last_ref[e] >= 0)
            def _():
                zero_block(last_ref[e])

        def tail(b, c):
            zero_block(b * EXPERT_BLOCK)
            return c

        lax.fori_loop(last_ref[N_EXPERTS] // EXPERT_BLOCK, xg_hbm.shape[0] // EXPERT_BLOCK, tail, 0)

    s_iota = lax.broadcasted_iota(jnp.int32, (nloc, tm), 0)
    hit = s_iota == lpos_ref[0:1, :]
    for k in range(1, TOP_K):
        hit = jnp.logical_or(hit, s_iota == lpos_ref[k:k + 1, :])
    perm = jnp.where(hit, 1.0, 0.0).astype(BF16)
    xs_ref[slot] = jnp.dot(perm, h_ref[...].astype(BF16), preferred_element_type=F32)

    def copies(t, s):
        return _segment_copies(n8_ref, lo_ref, gs_ref, t,
                               lambda r, n: xs_ref.at[s, pl.ds(r, n), :],
                               lambda r, n: xg_hbm.at[pl.ds(r, n), :], sem.at[s], True)

    copies(i, slot)(lambda d: d.start())

    @pl.when(i > 0)
    def _():
        copies(i - 1, 1 - slot)(lambda d: d.wait())

    @pl.when(i == pl.num_programs(0) - 1)
    def _():
        copies(i, slot)(lambda d: d.wait())


def _dispatch(last_blk, n8f, lof, gsf, lpos, h1, nrows, tm):
    ntile = h1.shape[0] // tm
    nloc = TOP_K * tm + N_EXPERTS * (SEG_ALIGN - 1)
    nloc = -(-nloc // SEG_ALIGN) * SEG_ALIGN
    grid_spec = pltpu.PrefetchScalarGridSpec(
        num_scalar_prefetch=4,
        grid=(ntile,),
        in_specs=[pl.BlockSpec((TOP_K, tm), lambda i, *_: (0, i)),
                  pl.BlockSpec((tm, D_MODEL), lambda i, *_: (i, 0))],
        out_specs=pl.BlockSpec(memory_space=pl.ANY),
        scratch_shapes=[pltpu.VMEM((2, nloc, D_MODEL), F32),
                        pltpu.VMEM((EXPERT_BLOCK, D_MODEL), F32),
                        pltpu.SemaphoreType.DMA, pltpu.SemaphoreType.DMA((2,))],
    )
    return pl.pallas_call(
        _dispatch_kernel,
        out_shape=jax.ShapeDtypeStruct((nrows, D_MODEL), F32),
        grid_spec=grid_spec,
        compiler_params=pltpu.CompilerParams(
            dimension_semantics=("arbitrary",), vmem_limit_bytes=VMEM_LIMIT),
        name="dispatch",
    )(last_blk, n8f, lof, gsf, lpos, h1)


GU_GROUP = 2 * LANES


def _wprep_kernel(w_ref, o_ref):
    r = lax.broadcasted_iota(jnp.int32, (GU_GROUP, GU_GROUP), 0)
    c = lax.broadcasted_iota(jnp.int32, (GU_GROUP, GU_GROUP), 1)
    perm = (r == jnp.where(c < LANES, 2 * c, 2 * (c - LANES) + 1)).astype(BF16)
    for m in range(2 * D_FF // GU_GROUP):
        cols = slice(m * GU_GROUP, (m + 1) * GU_GROUP)
        o_ref[0, :, cols] = jnp.dot(w_ref[0, :, cols].astype(BF16), perm,
                                    preferred_element_type=F32).astype(BF16)


def _wprep(w_gu):
    ne = w_gu.shape[0]
    return pl.pallas_call(
        _wprep_kernel,
        out_shape=jax.ShapeDtypeStruct(w_gu.shape, BF16),
        grid=(ne,),
        in_specs=[pl.BlockSpec((1, D_MODEL, 2 * D_FF), lambda e: (e, 0, 0))],
        out_specs=pl.BlockSpec((1, D_MODEL, 2 * D_FF), lambda e: (e, 0, 0)),
        compiler_params=pltpu.CompilerParams(
            dimension_semantics=("arbitrary",), vmem_limit_bytes=VMEM_LIMIT),
        name="wprep",
    )(w_gu)


def _ffn_kernel(be_ref, nu_ref, x_ref, wgu_ref, bgu_ref, wd_ref, bd_ref, y_ref):
    del be_ref

    @pl.when(pl.program_id(0) < nu_ref[0])
    def _():
        hgu = jnp.dot(x_ref[...].astype(BF16), wgu_ref[0], preferred_element_type=F32) + bgu_ref[0]
        ngrp = 2 * D_FF // GU_GROUP
        g = jnp.concatenate([hgu[:, m * GU_GROUP:m * GU_GROUP + LANES] for m in range(ngrp)], axis=-1)
        up = jnp.concatenate([hgu[:, m * GU_GROUP + LANES:(m + 1) * GU_GROUP] for m in range(ngrp)],
                             axis=-1)
        g = jnp.minimum(g, SWIGLU_LIMIT)
        up = jnp.clip(up, -SWIGLU_LIMIT, SWIGLU_LIMIT)
        act = (up + 1.0) * (g * jax.nn.sigmoid(g * SWIGLU_ALPHA))
        y_ref[...] = jnp.dot(act.astype(BF16), wd_ref[0], preferred_element_type=F32) + bd_ref[0]

    @pl.when(pl.program_id(0) >= nu_ref[0])
    def _():
        y_ref[...] = jnp.zeros_like(y_ref)


def _ffn(block_e, n_used, xg, wgu, bgu, wd, bd):
    nb = xg.shape[0] // EXPERT_BLOCK
    blk = lambda i, be, nu: (jnp.minimum(i, nu[0] - 1), 0)
    wsel = lambda i, be, nu: (be[jnp.minimum(i, nu[0] - 1)], 0, 0)
    grid_spec = pltpu.PrefetchScalarGridSpec(
        num_scalar_prefetch=2,
        grid=(nb,),
        in_specs=[
            pl.BlockSpec((EXPERT_BLOCK, D_MODEL), blk),
            pl.BlockSpec((1, D_MODEL, 2 * D_FF), wsel),
            pl.BlockSpec((1, 1, 2 * D_FF), wsel),
            pl.BlockSpec((1, D_FF, D_MODEL), wsel),
            pl.BlockSpec((1, 1, D_MODEL), wsel),
        ],
        out_specs=pl.BlockSpec((EXPERT_BLOCK, D_MODEL), lambda i, be, nu: (i, 0)),
    )
    return pl.pallas_call(
        _ffn_kernel,
        out_shape=jax.ShapeDtypeStruct(xg.shape, F32),
        grid_spec=grid_spec,
        compiler_params=pltpu.CompilerParams(
            dimension_semantics=("arbitrary",), vmem_limit_bytes=VMEM_LIMIT),
        name="ffn",
    )(block_e, n_used, xg, wgu, bgu, wd, bd)


def _combine_kernel(n8_ref, lo_ref, gs_ref, yb_hbm, lpos_ref, gate_ref, h1_ref, g2_ref, b2_ref,
                    outa_ref, outb_ref, ys_ref, sem, *, tiles_a):
    tm = h1_ref.shape[0]
    nloc = ys_ref.shape[1]
    i = pl.program_id(0)
    slot = i % 2

    def copies(t, s):
        return _segment_copies(n8_ref, lo_ref, gs_ref, t,
                               lambda r, n: ys_ref.at[s, pl.ds(r, n), :],
                               lambda r, n: yb_hbm.at[pl.ds(r, n), :], sem.at[s], False)

    @pl.when(i == 0)
    def _():
        ys_ref[...] = jnp.zeros_like(ys_ref)
        copies(0, 0)(lambda d: d.start())

    @pl.when(i + 1 < pl.num_programs(0))
    def _():
        copies(i + 1, 1 - slot)(lambda d: d.start())

    copies(i, slot)(lambda d: d.wait())

    s_iota = lax.broadcasted_iota(jnp.int32, (tm, nloc), 1)
    w = jnp.zeros((tm, nloc), F32)
    for k in range(TOP_K):
        w = jnp.where(s_iota == lpos_ref[:, k:k + 1], gate_ref[:, k:k + 1], w)
    moe = jnp.dot(w.astype(BF16), ys_ref[slot].astype(BF16), preferred_element_type=F32)
    out = _ln(ALPHA * h1_ref[...] + moe, g2_ref[...], b2_ref[...])

    @pl.when(i < tiles_a)
    def _():
        outa_ref[...] = out

    @pl.when(i >= tiles_a)
    def _():
        outb_ref[...] = out


def _combine(n8f, lof, gsf, ybuf, lpos_t, gate_t, h1, g2, b2, tm, n_a):
    n = h1.shape[0]
    tiles_a = n_a // tm
    nloc = TOP_K * tm + N_EXPERTS * (SEG_ALIGN - 1)
    nloc = -(-nloc // SEG_ALIGN) * SEG_ALIGN
    grid_spec = pltpu.PrefetchScalarGridSpec(
        num_scalar_prefetch=3,
        grid=(n // tm,),
        in_specs=[
            pl.BlockSpec(memory_space=pl.ANY),
            pl.BlockSpec((tm, TOP_K), lambda i, *_: (i, 0)),
            pl.BlockSpec((tm, TOP_K), lambda i, *_: (i, 0)),
            pl.BlockSpec((tm, D_MODEL), lambda i, *_: (i, 0)),
            pl.BlockSpec((1, D_MODEL), lambda i, *_: (0, 0)),
            pl.BlockSpec((1, D_MODEL), lambda i, *_: (0, 0)),
        ],
        out_specs=(pl.BlockSpec((tm, D_MODEL), lambda i, *_: (jnp.minimum(i, tiles_a - 1), 0)),
                   pl.BlockSpec((tm, D_MODEL), lambda i, *_: (jnp.maximum(i - tiles_a, 0), 0))),
        scratch_shapes=[pltpu.VMEM((2, nloc, D_MODEL), F32), pltpu.SemaphoreType.DMA((2,))],
    )
    return pl.pallas_call(
        functools.partial(_combine_kernel, tiles_a=tiles_a),
        out_shape=(jax.ShapeDtypeStruct((n_a, D_MODEL), F32),
                   jax.ShapeDtypeStruct((n - n_a, D_MODEL), F32)),
        grid_spec=grid_spec,
        compiler_params=pltpu.CompilerParams(
            dimension_semantics=("arbitrary",), vmem_limit_bytes=VMEM_LIMIT),
        name="combine",
    )(n8f, lof, gsf, ybuf, lpos_t, gate_t, h1, g2, b2)


def _pick_tile(t, pref):
    return pref if t % pref == 0 else t


def kernel(x_prompt, x_sample, cache_meta_k, cache_meta_v, cache_win_k, cache_win_v, state_conv,
           meta_tokens, ln_in_g, ln_in_b, w_in, b_in, conv_w, attn_sinks, w_attn_br, w_conv_br, w_o,
           ln1_g, ln1_b, w_router, b_router, w_gu, b_gu, w_d, b_d, ln2_g, ln2_b):
    bp, tp, _ = x_prompt.shape
    bs, ts, _ = x_sample.shape
    row2 = lambda a: a.reshape(1, -1)

    w_in_b = w_in[0].astype(BF16)
    b_in_r = row2(b_in[0])
    gin, bin_ = row2(ln_in_g), row2(ln_in_b)
    wa, wc, wo = w_attn_br[0].astype(BF16), w_conv_br[0].astype(BF16), w_o[0].astype(BF16)
    wr_t = w_router[0].T.astype(BF16)
    br = b_router[0].reshape(N_EXPERTS, 1)
    wgu = _wprep(w_gu[0])
    bgu = (b_gu[0].reshape(N_EXPERTS, 2 * D_FF // GU_GROUP, LANES, 2).transpose(0, 1, 3, 2)
           .reshape(N_EXPERTS, 1, 2 * D_FF))
    wd = w_d[0].astype(BF16)
    bd = b_d[0][:, None, :]
    sinks = attn_sinks[0]

    zbuf = jnp.zeros((1, CONV_W - 1, D_CONV), F32)
    _, _, _, _, km, vm, um_last = _inproj(meta_tokens[None], zbuf, w_in_b, b_in_r, conv_w[0],
                                          gin, bin_, N_META)

    n_total = bp * tp + bs * ts

    def stream(x, cbuf, past_k, past_v, mk, mv, past_valid, tm_pref, row_off, prev):
        bsz, t, _ = x.shape
        q, ga, gc, oc, k, v, u_last = _inproj(x, cbuf, w_in_b, b_in_r, conv_w[0], gin, bin_,
                                             _pick_tile(t, tm_pref))
        tpad = -(-t // CHUNK) * CHUNK
        padq = ((0, 0), (0, tpad - t), (0, 0))
        kfull = jnp.concatenate([past_k, jnp.pad(k, padq)], axis=1).astype(BF16)
        vfull = jnp.concatenate([past_v, jnp.pad(v, padq)], axis=1).astype(BF16)
        oa = _attention(jnp.pad(q, padq), kfull, vfull, mk.astype(BF16), mv.astype(BF16), sinks,
                        past_valid, t, _pick_tile(tpad, tm_pref))[:, :t]
        n = bsz * t
        flat = lambda a: a.reshape(n, a.shape[-1])
        h1, idx, gate = _mix(flat(x), flat(oa), flat(oc), flat(ga), flat(gc), wa, wc, wo,
                                  gin, bin_, row2(ln1_g[0]), row2(ln1_b[0]), wr_t, br,
                                  _pick_tile(n, tm_pref), n_total, row_off, prev)
        return h1, idx, gate, k, v, u_last

    zk = jnp.zeros((bp, WINDOW, KV_W), F32)
    h1p, idxp, gatep, kp, vp, ulp = stream(x_prompt, um_last, zk, zk, km, vm, False, 512, 0, None)
    wk_s = cache_win_k[0].reshape(bs, WINDOW, KV_W)
    wv_s = cache_win_v[0].reshape(bs, WINDOW, KV_W)
    h1, idx, gate, ks, vs, uls = stream(
        x_sample, state_conv[0], wk_s, wv_s, cache_meta_k[0].reshape(bs, N_META, KV_W),
        cache_meta_v[0].reshape(bs, N_META, KV_W), True, 512, bp * tp, (h1p, idxp, gatep))

    n = n_total
    tmm = EXPERT_BLOCK
    ntile = n // tmm
    lpos, n8_tab = _plan(idx, tmm)
    t8 = n8_tab[:, :, 0].astype(jnp.int32)
    padded = (jnp.sum(t8, axis=0) + EXPERT_BLOCK - 1) // EXPERT_BLOCK * EXPERT_BLOCK
    pend = jnp.cumsum(padded)
    poff = pend - padded
    gstart = poff[None, :] + jnp.cumsum(t8, axis=0) - t8
    lstart = jnp.cumsum(t8, axis=1) - t8
    nb = -(-(n * TOP_K + ntile * N_EXPERTS * (SEG_ALIGN - 1)) // EXPERT_BLOCK) + N_EXPERTS
    block_e = jnp.minimum(
        jnp.sum((pend[None, :] <= (jnp.arange(nb) * EXPERT_BLOCK)[:, None]).astype(jnp.int32), axis=1),
        N_EXPERTS - 1).astype(jnp.int32)
    n_used = (pend[-1:] // EXPERT_BLOCK).astype(jnp.int32)
    last_blk = jnp.concatenate([jnp.where(padded > 0, pend - EXPERT_BLOCK, -1), pend[-1:]]).astype(jnp.int32)
    n8f, lof, gsf = t8.reshape(-1), lstart.reshape(-1).astype(jnp.int32), gstart.reshape(-1).astype(jnp.int32)

    xg = _dispatch(last_blk, n8f, lof, gsf, lpos, h1, nb * EXPERT_BLOCK, tmm)
    ybuf = _ffn(block_e, n_used, xg, wgu, bgu, wd, bd)
    y_p, y_s = _combine(n8f, lof, gsf, ybuf, lpos.T, gate.T, h1, row2(ln2_g[0]), row2(ln2_b[0]),
                        tmm, bp * tp)
    y_prompt = y_p.reshape(bp, tp, D_MODEL)
    y_sample = y_s.reshape(bs, ts, D_MODEL)
    kv5 = lambda a: a.reshape(a.shape[0], a.shape[1], N_KV, HEAD_DIM)[None]
    mk_p = jnp.broadcast_to(kv5(km), (1, bp, N_META, N_KV, HEAD_DIM))
    mv_p = jnp.broadcast_to(kv5(vm), (1, bp, N_META, N_KV, HEAD_DIM))
    wk_p = kv5(kp[:, -WINDOW:])
    wv_p = kv5(vp[:, -WINDOW:])
    wk_o = kv5(jnp.concatenate([wk_s, ks], axis=1)[:, -WINDOW:])
    wv_o = kv5(jnp.concatenate([wv_s, vs], axis=1)[:, -WINDOW:])
    return (y_prompt, y_sample, mk_p, mv_p, wk_p, wv_p, ulp[None], wk_o, wv_o, uls[None])
```

```python
import functools

import jax
import jax.numpy as jnp
from jax import lax
from jax.experimental import pallas as pl
from jax.experimental.pallas import tpu as pltpu

D_MODEL = 1024
CHUNK = 64
N_META = 16
N_Q = 16
N_KV = 4
GROUP = N_Q // N_KV
HEAD_DIM = 64
ATTN_W = N_Q * HEAD_DIM
KV_W = N_KV * HEAD_DIM
WINDOW = 128
D_CONV = D_MODEL
CONV_W = 3
N_EXPERTS = 32
TOP_K = 4
D_FF = D_MODEL
SWIGLU_LIMIT = 7.0
SWIGLU_ALPHA = 1.702
EXPERT_BLOCK = 512
TOKEN_TILE = 256
LN_EPS = 1e-5
DEPTH = 1
ALPHA = (2 * DEPTH) ** 0.25
ATTN_SCALE = HEAD_DIM ** -0.5

OFF_GA = ATTN_W
OFF_GC = OFF_GA + D_MODEL
OFF_CB = OFF_GC + D_MODEL
OFF_K = OFF_CB + D_CONV
OFF_V = OFF_K + KV_W
OFF_CC = OFF_V + KV_W
OFF_CH = OFF_CC + D_CONV
IN_COLS = OFF_CH + D_CONV

LANES = 128
VMEM_LIMIT = 56 * 1024 * 1024

F32 = jnp.float32
BF16 = jnp.bfloat16


def _ln(x, g, b):
    mu = jnp.mean(x, axis=-1, keepdims=True)
    xc = x - mu
    var = jnp.mean(xc * xc, axis=-1, keepdims=True)
    return (xc * lax.rsqrt(var + LN_EPS)) * g + b


def _const_spec(shape):
    nd = len(shape)
    return pl.BlockSpec(shape, lambda *_: (0,) * nd, pipeline_mode=pl.Buffered(1))


def _inproj_kernel(x_ref, cbuf_ref, w_ref, b_ref, cw_ref, g_ref, be_ref,
                   q_ref, ga_ref, gc_ref, oc_ref, k_ref, v_ref, ul_ref, carry_ref):
    tm = x_ref.shape[1]

    @pl.when(pl.program_id(1) == 0)
    def _():
        carry_ref[...] = cbuf_ref[0]

    h = _ln(x_ref[0], g_ref[...], be_ref[...]).astype(BF16)

    def proj(off, width):
        return (jnp.dot(h, w_ref[:, off:off + width], preferred_element_type=F32)
                + b_ref[:, off:off + width])

    q_ref[0] = (proj(0, ATTN_W) * ATTN_SCALE).astype(BF16)
    ga_ref[0] = jax.nn.sigmoid(proj(OFF_GA, D_MODEL)).astype(BF16)
    gc_ref[0] = jax.nn.sigmoid(proj(OFF_GC, D_MODEL)).astype(BF16)
    k_ref[0] = proj(OFF_K, KV_W)
    v_ref[0] = proj(OFF_V, KV_W)

    u = proj(OFF_CC, D_CONV) * proj(OFF_CH, D_CONV)
    prev = carry_ref[...]
    row = lax.broadcasted_iota(jnp.int32, (tm, D_CONV), 0)
    u1 = jnp.where(row == 0, prev[1:2], pltpu.roll(u, 1, 0))
    u2 = jnp.where(row == 0, prev[0:1],
                   jnp.where(row == 1, prev[1:2], pltpu.roll(u, 2, 0)))
    conv = cw_ref[0:1] * u2 + cw_ref[1:2] * u1 + cw_ref[2:3] * u
    oc_ref[0] = (proj(OFF_CB, D_CONV) * conv).astype(BF16)
    last = u[tm - (CONV_W - 1):tm]
    carry_ref[...] = last
    ul_ref[0] = last


def _inproj(x, cbuf, w_in, b_in, conv_w, ln_g, ln_b, tm):
    bsz, t, _ = x.shape
    cb_map = (lambda b, i: (b, 0, 0)) if cbuf.shape[0] == bsz else (lambda b, i: (0, 0, 0))
    row_spec = lambda w: pl.BlockSpec((1, tm, w), lambda b, i: (b, i, 0))
    outs = (
        jax.ShapeDtypeStruct((bsz, t, ATTN_W), BF16),
        jax.ShapeDtypeStruct((bsz, t, D_MODEL), BF16),
        jax.ShapeDtypeStruct((bsz, t, D_MODEL), BF16),
        jax.ShapeDtypeStruct((bsz, t, D_CONV), BF16),
        jax.ShapeDtypeStruct((bsz, t, KV_W), F32),
        jax.ShapeDtypeStruct((bsz, t, KV_W), F32),
        jax.ShapeDtypeStruct((bsz, CONV_W - 1, D_CONV), F32),
    )
    return pl.pallas_call(
        _inproj_kernel,
        out_shape=outs,
        grid=(bsz, t // tm),
        in_specs=[
            row_spec(D_MODEL),
            pl.BlockSpec((1, CONV_W - 1, D_CONV), cb_map),
            _const_spec((D_MODEL, IN_COLS)),
            _const_spec((1, IN_COLS)),
            _const_spec((CONV_W, D_CONV)),
            _const_spec((1, D_MODEL)),
            _const_spec((1, D_MODEL)),
        ],
        out_specs=(
            row_spec(ATTN_W), row_spec(D_MODEL), row_spec(D_MODEL), row_spec(D_CONV),
            row_spec(KV_W), row_spec(KV_W),
            pl.BlockSpec((1, CONV_W - 1, D_CONV), lambda b, i: (b, 0, 0)),
        ),
        scratch_shapes=[pltpu.VMEM((CONV_W - 1, D_CONV), F32)],
        compiler_params=pltpu.CompilerParams(
            dimension_semantics=("arbitrary", "arbitrary"), vmem_limit_bytes=VMEM_LIMIT),
        name="inproj",
    )(x, cbuf, w_in, b_in, conv_w, ln_g, ln_b)


def _attn_kernel(sink_ref, q_ref, k_ref, v_ref, mk_ref, mv_ref, o_ref, *, past_valid, t_valid):
    tq = q_ref.shape[1]
    nchunk = tq // CHUNK
    nkeys = WINDOW + CHUNK
    unroll = 2 if nchunk % 2 == 0 else 1
    tile = pl.program_id(1)
    mk = mk_ref[0]
    mv = mv_ref[0]
    contract = (((1,), (1,)), ((), ()))
    qrow = lax.broadcasted_iota(jnp.int32, (GROUP * CHUNK, 1), 0)
    col = lax.broadcasted_iota(jnp.int32, (1, N_META + nkeys), 1)

    def one_chunk(ci):
        row0 = pl.multiple_of((tile * nchunk + ci) * CHUNK, CHUNK)
        qoff = pl.multiple_of(ci * CHUNK, CHUNK)
        kc = jnp.concatenate([mk, k_ref[0, pl.ds(row0, nkeys), :]], axis=0)
        vc = jnp.concatenate([mv, v_ref[0, pl.ds(row0, nkeys), :]], axis=0)
        qc = q_ref[0, pl.ds(qoff, CHUNK), :]
        kpos = row0 + col - N_META
        valid = kpos < WINDOW + t_valid
        if not past_valid:
            valid = jnp.logical_and(valid, kpos >= WINDOW)
        valid = jnp.logical_or(valid, col < N_META)
        for g in range(N_KV):
            heads = [qc[:, (g * GROUP + i) * HEAD_DIM:(g * GROUP + i + 1) * HEAD_DIM]
                     for i in range(GROUP)]
            qg = jnp.concatenate(heads, axis=0)
            ksl = slice(g * HEAD_DIM, (g + 1) * HEAD_DIM)
            s = lax.dot_general(qg, kc[:, ksl], contract, preferred_element_type=F32)
            s = jnp.where(valid, s, -jnp.inf)
            sink = jnp.full((GROUP * CHUNK, 1), sink_ref[g * GROUP], F32)
            for i in range(1, GROUP):
                sink = jnp.where(qrow >= i * CHUNK, sink_ref[g * GROUP + i], sink)
            m = jnp.maximum(jnp.max(s, axis=-1, keepdims=True), sink)
            p = jnp.exp(s - m)
            den = jnp.sum(p, axis=-1, keepdims=True) + jnp.exp(sink - m)
            o = jnp.dot((p / den).astype(BF16), vc[:, ksl], preferred_element_type=F32)
            for i in range(GROUP):
                h = g * GROUP + i
                o_ref[0, pl.ds(qoff, CHUNK),
                      h * HEAD_DIM:(h + 1) * HEAD_DIM] = o[i * CHUNK:(i + 1) * CHUNK].astype(BF16)

    def body(it, carry):
        for u in range(unroll):
            one_chunk(it * unroll + u)
        return carry

    lax.fori_loop(0, nchunk // unroll, body, 0)


def _attention(q, kfull, vfull, mk, mv, sinks, past_valid, t_valid, tq):
    bsz, t, _ = q.shape
    tk = kfull.shape[1]
    m_map = (lambda b, i: (b, 0, 0)) if mk.shape[0] == bsz else (lambda b, i: (0, 0, 0))
    return pl.pallas_call(
        functools.partial(_attn_kernel, past_valid=past_valid, t_valid=t_valid),
        out_shape=jax.ShapeDtypeStruct((bsz, t, ATTN_W), BF16),
        grid=(bsz, t // tq),
        in_specs=[
            pl.BlockSpec(memory_space=pltpu.SMEM),
            pl.BlockSpec((1, tq, ATTN_W), lambda b, i: (b, i, 0)),
            pl.BlockSpec((1, tk, KV_W), lambda b, i: (b, 0, 0)),
            pl.BlockSpec((1, tk, KV_W), lambda b, i: (b, 0, 0)),
            pl.BlockSpec((1, N_META, KV_W), m_map),
            pl.BlockSpec((1, N_META, KV_W), m_map),
        ],
        out_specs=pl.BlockSpec((1, tq, ATTN_W), lambda b, i: (b, i, 0)),
        compiler_params=pltpu.CompilerParams(
            dimension_semantics=("arbitrary", "arbitrary"), vmem_limit_bytes=VMEM_LIMIT),
        name="attn",
    )(sinks, q, kfull, vfull, mk, mv)


def _mix_kernel(x_ref, oa_ref, oc_ref, ga_ref, gc_ref, wa_ref, wc_ref, wo_ref,
                gin_ref, bin_ref, g1_ref, b1_ref, wr_ref, br_ref,
                h1_ref, idx_ref, gate_ref):
    tm = x_ref.shape[0]
    h0 = _ln(x_ref[...], gin_ref[...], bin_ref[...])
    a = jnp.dot(oa_ref[...], wa_ref[...], preferred_element_type=F32)
    c = jnp.dot(oc_ref[...], wc_ref[...], preferred_element_type=F32)
    mixed = ga_ref[...].astype(F32) * a + gc_ref[...].astype(F32) * c
    y = jnp.dot(mixed.astype(BF16), wo_ref[...], preferred_element_type=F32)
    h1 = _ln(ALPHA * h0 + y, g1_ref[...], b1_ref[...])
    h1_ref[...] = h1

    logits = lax.dot_general(wr_ref[...], h1.astype(BF16), (((1,), (1,)), ((), ())),
                             preferred_element_type=F32) + br_ref[...]
    e_iota = lax.broadcasted_iota(jnp.int32, (N_EXPERTS, tm), 0)
    vals, idxs = [], []
    for _ in range(TOP_K):
        mx = jnp.max(logits, axis=0, keepdims=True)
        ix = jnp.min(jnp.where(logits == mx, e_iota, N_EXPERTS), axis=0, keepdims=True)
        sel = e_iota == ix
        vals.append(mx)
        idxs.append(ix)
        logits = jnp.where(sel, -jnp.inf, logits)
    ev = [jnp.exp(v - vals[0]) for v in vals]
    den = ev[0] + ev[1] + ev[2] + ev[3]
    idx_ref[...] = jnp.concatenate(idxs, axis=0)
    gate_ref[...] = jnp.concatenate([e / den for e in ev], axis=0)


def _mix(x, oa, oc, ga, gc, wa, wc, wo, gin, bin_, g1, b1, wr_t, br, tm, n_total, row_off, prev):
    n = x.shape[0]
    off = row_off // tm
    outs = (
        jax.ShapeDtypeStruct((n_total, D_MODEL), F32),
        jax.ShapeDtypeStruct((TOP_K, n_total), jnp.int32),
        jax.ShapeDtypeStruct((TOP_K, n_total), F32),
    )
    nprev = 0 if prev is None else len(prev)
    ntile = n // tm
    extra = 0 if prev is not None else -(-(n_total - n) // tm)
    row = lambda w: pl.BlockSpec((tm, w), lambda i: (jnp.minimum(i, ntile - 1), 0))

    def kern(*refs):
        refs = refs[nprev:]
        if extra == 0:
            _mix_kernel(*refs)
            return

        @pl.when(pl.program_id(0) < ntile)
        def _():
            _mix_kernel(*refs)

        @pl.when(pl.program_id(0) >= ntile)
        def _():
            for o_ref in refs[-3:]:
                o_ref[...] = jnp.zeros_like(o_ref)

    return pl.pallas_call(
        kern,
        out_shape=outs,
        grid=(ntile + extra,),
        in_specs=[pl.BlockSpec(memory_space=pl.ANY)] * nprev + [
            row(D_MODEL), row(ATTN_W), row(D_CONV), row(D_MODEL), row(D_MODEL),
            _const_spec((ATTN_W, D_MODEL)), _const_spec((D_CONV, D_MODEL)),
            _const_spec((D_MODEL, D_MODEL)),
            _const_spec((1, D_MODEL)), _const_spec((1, D_MODEL)),
            _const_spec((1, D_MODEL)), _const_spec((1, D_MODEL)),
            _const_spec((N_EXPERTS, D_MODEL)), _const_spec((N_EXPERTS, 1)),
        ],
        out_specs=(
            pl.BlockSpec((tm, D_MODEL), lambda i: (i + off, 0)),
            pl.BlockSpec((TOP_K, tm), lambda i: (0, i + off)),
            pl.BlockSpec((TOP_K, tm), lambda i: (0, i + off)),
        ),
        input_output_aliases={j: j for j in range(nprev)},
        compiler_params=pltpu.CompilerParams(
            dimension_semantics=("arbitrary",), vmem_limit_bytes=VMEM_LIMIT),
        name="mix",
    )(*(prev or ()), x, oa, oc, ga, gc, wa, wc, wo, gin, bin_, g1, b1, wr_t, br)


SEG_ALIGN = 8


def _plan_kernel(idx_ref, lpos_ref, n8_ref):
    tm = idx_ref.shape[1]
    e_iota = lax.broadcasted_iota(jnp.int32, (N_EXPERTS, tm), 0)
    sels = [e_iota == idx_ref[k:k + 1, :] for k in range(TOP_K)]
    oh = sels[0].astype(F32)
    for k in range(1, TOP_K):
        oh = oh + sels[k].astype(F32)
    cnt = jnp.sum(oh, axis=1, keepdims=True)
    n8 = jnp.floor((cnt + (SEG_ALIGN - 1)) * (1.0 / SEG_ALIGN)) * SEG_ALIGN
    n8b = jnp.broadcast_to(n8, (N_EXPERTS, LANES))
    low = (lax.broadcasted_iota(jnp.int32, (N_EXPERTS, N_EXPERTS), 0)
           > lax.broadcasted_iota(jnp.int32, (N_EXPERTS, N_EXPERTS), 1)).astype(BF16)
    loff = jnp.dot(low, n8b.astype(BF16), preferred_element_type=F32)[:, 0:1]
    tri = (lax.broadcasted_iota(jnp.int32, (tm, tm), 0)
           < lax.broadcasted_iota(jnp.int32, (tm, tm), 1)).astype(BF16)
    pos = jnp.dot(oh.astype(BF16), tri, preferred_element_type=F32) + loff
    lpos = [jnp.sum(jnp.where(s, pos, 0.0), axis=0, keepdims=True) for s in sels]
    lpos_ref[...] = jnp.concatenate(lpos, axis=0).astype(jnp.int32)
    n8_ref[0] = n8b


def _plan(idx, tm):
    n = idx.shape[1]
    return pl.pallas_call(
        _plan_kernel,
        out_shape=(jax.ShapeDtypeStruct((TOP_K, n), jnp.int32),
                   jax.ShapeDtypeStruct((n // tm, N_EXPERTS, LANES), F32)),
        grid=(n // tm,),
        in_specs=[pl.BlockSpec((TOP_K, tm), lambda i: (0, i))],
        out_specs=(pl.BlockSpec((TOP_K, tm), lambda i: (0, i)),
                   pl.BlockSpec((1, N_EXPERTS, LANES), lambda i: (i, 0, 0))),
        compiler_params=pltpu.CompilerParams(dimension_semantics=("arbitrary",)),
        name="plan",
    )(idx)


def _segment_copies(n8_ref, lo_ref, gs_ref, t, local_of, global_of, sem, to_global):
    def run(action):
        for e in range(N_EXPERTS):
            n = n8_ref[t * N_EXPERTS + e]

            @pl.when(n > 0)
            def _():
                rows = pl.multiple_of(n, SEG_ALIGN)
                loc = local_of(pl.multiple_of(lo_ref[t * N_EXPERTS + e], SEG_ALIGN), rows)
                glo = global_of(pl.multiple_of(gs_ref[t * N_EXPERTS + e], SEG_ALIGN), rows)
                src, dst = (loc, glo) if to_global else (glo, loc)
                action(pltpu.make_async_copy(src, dst, sem))
    return run


def _dispatch_kernel(last_ref, n8_ref, lo_ref, gs_ref, lpos_ref, h_ref, xg_hbm, xs_ref, zero_ref,
                     zsem, sem):
    tm = h_ref.shape[0]
    nloc = xs_ref.shape[1]
    i = pl.program_id(0)
    slot = i % 2

    @pl.when(i == 0)
    def _():
        zero_ref[...] = jnp.zeros_like(zero_ref)

        def zero_block(row):
            blk = pl.multiple_of(row, EXPERT_BLOCK)
            z = pltpu.make_async_copy(zero_ref, xg_hbm.at[pl.ds(blk, EXPERT_BLOCK), :], zsem)
            z.start()
            z.wait()

        for e in range(N_EXPERTS):
            @pl.when(last_ref[e] >= 0)
            def _():
                zero_block(last_ref[e])

        def tail(b, c):
            zero_block(b * EXPERT_BLOCK)
            return c

        lax.fori_loop(last_ref[N_EXPERTS] // EXPERT_BLOCK, xg_hbm.shape[0] // EXPERT_BLOCK, tail, 0)

    s_iota = lax.broadcasted_iota(jnp.int32, (nloc, tm), 0)
    hit = s_iota == lpos_ref[0:1, :]
    for k in range(1, TOP_K):
        hit = jnp.logical_or(hit, s_iota == lpos_ref[k:k + 1, :])
    perm = jnp.where(hit, 1.0, 0.0).astype(BF16)
    xs_ref[slot] = jnp.dot(perm, h_ref[...].astype(BF16), preferred_element_type=F32)

    def copies(t, s):
        return _segment_copies(n8_ref, lo_ref, gs_ref, t,
                               lambda r, n: xs_ref.at[s, pl.ds(r, n), :],
                               lambda r, n: xg_hbm.at[pl.ds(r, n), :], sem.at[s], True)

    copies(i, slot)(lambda d: d.start())

    @pl.when(i > 0)
    def _():
        copies(i - 1, 1 - slot)(lambda d: d.wait())

    @pl.when(i == pl.num_programs(0) - 1)
    def _():
        copies(i, slot)(lambda d: d.wait())


def _dispatch(last_blk, n8f, lof, gsf, lpos, h1, nrows, tm):
    ntile = h1.shape[0] // tm
    nloc = TOP_K * tm + N_EXPERTS * (SEG_ALIGN - 1)
    nloc = -(-nloc // SEG_ALIGN) * SEG_ALIGN
    grid_spec = pltpu.PrefetchScalarGridSpec(
        num_scalar_prefetch=4,
        grid=(ntile,),
        in_specs=[pl.BlockSpec((TOP_K, tm), lambda i, *_: (0, i)),
                  pl.BlockSpec((tm, D_MODEL), lambda i, *_: (i, 0))],
        out_specs=pl.BlockSpec(memory_space=pl.ANY),
        scratch_shapes=[pltpu.VMEM((2, nloc, D_MODEL), F32),
                        pltpu.VMEM((EXPERT_BLOCK, D_MODEL), F32),
                        pltpu.SemaphoreType.DMA, pltpu.SemaphoreType.DMA((2,))],
    )
    return pl.pallas_call(
        _dispatch_kernel,
        out_shape=jax.ShapeDtypeStruct((nrows, D_MODEL), F32),
        grid_spec=grid_spec,
        compiler_params=pltpu.CompilerParams(
            dimension_semantics=("arbitrary",), vmem_limit_bytes=VMEM_LIMIT),
        name="dispatch",
    )(last_blk, n8f, lof, gsf, lpos, h1)


GU_GROUP = 2 * LANES


def _wprep_kernel(w_ref, o_ref):
    r = lax.broadcasted_iota(jnp.int32, (GU_GROUP, GU_GROUP), 0)
    c = lax.broadcasted_iota(jnp.int32, (GU_GROUP, GU_GROUP), 1)
    perm = (r == jnp.where(c < LANES, 2 * c, 2 * (c - LANES) + 1)).astype(BF16)
    for m in range(2 * D_FF // GU_GROUP):
        cols = slice(m * GU_GROUP, (m + 1) * GU_GROUP)
        o_ref[0, :, cols] = jnp.dot(w_ref[0, :, cols].astype(BF16), perm,
                                    preferred_element_type=F32).astype(BF16)


def _wprep(w_gu):
    ne = w_gu.shape[0]
    return pl.pallas_call(
        _wprep_kernel,
        out_shape=jax.ShapeDtypeStruct(w_gu.shape, BF16),
        grid=(ne,),
        in_specs=[pl.BlockSpec((1, D_MODEL, 2 * D_FF), lambda e: (e, 0, 0))],
        out_specs=pl.BlockSpec((1, D_MODEL, 2 * D_FF), lambda e: (e, 0, 0)),
        compiler_params=pltpu.CompilerParams(
            dimension_semantics=("arbitrary",), vmem_limit_bytes=VMEM_LIMIT),
        name="wprep",
    )(w_gu)


def _ffn_kernel(be_ref, nu_ref, x_ref, wgu_ref, bgu_ref, wd_ref, bd_ref, y_ref):
    del be_ref

    @pl.when(pl.program_id(0) < nu_ref[0])
    def _():
        hgu = jnp.dot(x_ref[...].astype(BF16), wgu_ref[0], preferred_element_type=F32) + bgu_ref[0]
        ngrp = 2 * D_FF // GU_GROUP
        g = jnp.concatenate([hgu[:, m * GU_GROUP:m * GU_GROUP + LANES] for m in range(ngrp)], axis=-1)
        up = jnp.concatenate([hgu[:, m * GU_GROUP + LANES:(m + 1) * GU_GROUP] for m in range(ngrp)],
                             axis=-1)
        g = jnp.minimum(g, SWIGLU_LIMIT)
        up = jnp.clip(up, -SWIGLU_LIMIT, SWIGLU_LIMIT)
        act = (up + 1.0) * (g * jax.nn.sigmoid(g * SWIGLU_ALPHA))
        y_ref[...] = jnp.dot(act.astype(BF16), wd_ref[0], preferred_element_type=F32) + bd_ref[0]

    @pl.when(pl.program_id(0) >= nu_ref[0])
    def _():
        y_ref[...] = jnp.zeros_like(y_ref)


def _ffn(block_e, n_used, xg, wgu, bgu, wd, bd):
    nb = xg.shape[0] // EXPERT_BLOCK
    blk = lambda i, be, nu: (jnp.minimum(i, nu[0] - 1), 0)
    wsel = lambda i, be, nu: (be[jnp.minimum(i, nu[0] - 1)], 0, 0)
    grid_spec = pltpu.PrefetchScalarGridSpec(
        num_scalar_prefetch=2,
        grid=(nb,),
        in_specs=[
            pl.BlockSpec((EXPERT_BLOCK, D_MODEL), blk),
            pl.BlockSpec((1, D_MODEL, 2 * D_FF), wsel),
            pl.BlockSpec((1, 1, 2 * D_FF), wsel),
            pl.BlockSpec((1, D_FF, D_MODEL), wsel),
            pl.BlockSpec((1, 1, D_MODEL), wsel),
        ],
        out_specs=pl.BlockSpec((EXPERT_BLOCK, D_MODEL), lambda i, be, nu: (i, 0)),
    )
    return pl.pallas_call(
        _ffn_kernel,
        out_shape=jax.ShapeDtypeStruct(xg.shape, F32),
        grid_spec=grid_spec,
        compiler_params=pltpu.CompilerParams(
            dimension_semantics=("arbitrary",), vmem_limit_bytes=VMEM_LIMIT),
        name="ffn",
    )(block_e, n_used, xg, wgu, bgu, wd, bd)


def _combine_kernel(n8_ref, lo_ref, gs_ref, yb_hbm, lpos_ref, gate_ref, h1_ref, g2_ref, b2_ref,
                    outa_ref, outb_ref, ys_ref, sem, *, tiles_a):
    tm = h1_ref.shape[0]
    nloc = ys_ref.shape[1]
    i = pl.program_id(0)
    slot = i % 2

    def copies(t, s):
        return _segment_copies(n8_ref, lo_ref, gs_ref, t,
                               lambda r, n: ys_ref.at[s, pl.ds(r, n), :],
                               lambda r, n: yb_hbm.at[pl.ds(r, n), :], sem.at[s], False)

    @pl.when(i == 0)
    def _():
        ys_ref[...] = jnp.zeros_like(ys_ref)
        copies(0, 0)(lambda d: d.start())

    @pl.when(i + 1 < pl.num_programs(0))
    def _():
        copies(i + 1, 1 - slot)(lambda d: d.start())

    copies(i, slot)(lambda d: d.wait())

    s_iota = lax.broadcasted_iota(jnp.int32, (tm, nloc), 1)
    w = jnp.zeros((tm, nloc), F32)
    for k in range(TOP_K):
        w = jnp.where(s_iota == lpos_ref[:, k:k + 1], gate_ref[:, k:k + 1], w)
    moe = jnp.dot(w.astype(BF16), ys_ref[slot].astype(BF16), preferred_element_type=F32)
    out = _ln(ALPHA * h1_ref[...] + moe, g2_ref[...], b2_ref[...])

    @pl.when(i < tiles_a)
    def _():
        outa_ref[...] = out

    @pl.when(i >= tiles_a)
    def _():
        outb_ref[...] = out


def _combine(n8f, lof, gsf, ybuf, lpos_t, gate_t, h1, g2, b2, tm, n_a):
    n = h1.shape[0]
    tiles_a = n_a // tm
    nloc = TOP_K * tm + N_EXPERTS * (SEG_ALIGN - 1)
    nloc = -(-nloc // SEG_ALIGN) * SEG_ALIGN
    grid_spec = pltpu.PrefetchScalarGridSpec(
        num_scalar_prefetch=3,
        grid=(n // tm,),
        in_specs=[
            pl.BlockSpec(memory_space=pl.ANY),
            pl.BlockSpec((tm, TOP_K), lambda i, *_: (i, 0)),
            pl.BlockSpec((tm, TOP_K), lambda i, *_: (i, 0)),
            pl.BlockSpec((tm, D_MODEL), lambda i, *_: (i, 0)),
            pl.BlockSpec((1, D_MODEL), lambda i, *_: (0, 0)),
            pl.BlockSpec((1, D_MODEL), lambda i, *_: (0, 0)),
        ],
        out_specs=(pl.BlockSpec((tm, D_MODEL), lambda i, *_: (jnp.minimum(i, tiles_a - 1), 0)),
                   pl.BlockSpec((tm, D_MODEL), lambda i, *_: (jnp.maximum(i - tiles_a, 0), 0))),
        scratch_shapes=[pltpu.VMEM((2, nloc, D_MODEL), F32), pltpu.SemaphoreType.DMA((2,))],
    )
    return pl.pallas_call(
        functools.partial(_combine_kernel, tiles_a=tiles_a),
        out_shape=(jax.ShapeDtypeStruct((n_a, D_MODEL), F32),
                   jax.ShapeDtypeStruct((n - n_a, D_MODEL), F32)),
        grid_spec=grid_spec,
        compiler_params=pltpu.CompilerParams(
            dimension_semantics=("arbitrary",), vmem_limit_bytes=VMEM_LIMIT),
        name="combine",
    )(n8f, lof, gsf, ybuf, lpos_t, gate_t, h1, g2, b2)


def _pick_tile(t, pref):
    return pref if t % pref == 0 else t


def kernel(x_prompt, x_sample, cache_meta_k, cache_meta_v, cache_win_k, cache_win_v, state_conv,
           meta_tokens, ln_in_g, ln_in_b, w_in, b_in, conv_w, attn_sinks, w_attn_br, w_conv_br, w_o,
           ln1_g, ln1_b, w_router, b_router, w_gu, b_gu, w_d, b_d, ln2_g, ln2_b):
    bp, tp, _ = x_prompt.shape
    bs, ts, _ = x_sample.shape
    row2 = lambda a: a.reshape(1, -1)

    w_in_b = w_in[0].astype(BF16)
    b_in_r = row2(b_in[0])
    gin, bin_ = row2(ln_in_g), row2(ln_in_b)
    wa, wc, wo = w_attn_br[0].astype(BF16), w_conv_br[0].astype(BF16), w_o[0].astype(BF16)
    wr_t = w_router[0].T.astype(BF16)
    br = b_router[0].reshape(N_EXPERTS, 1)
    wgu = _wprep(w_gu[0])
    bgu = (b_gu[0].reshape(N_EXPERTS, 2 * D_FF // GU_GROUP, LANES, 2).transpose(0, 1, 3, 2)
           .reshape(N_EXPERTS, 1, 2 * D_FF))
    wd = w_d[0].astype(BF16)
    bd = b_d[0][:, None, :]
    sinks = attn_sinks[0]

    zbuf = jnp.zeros((1, CONV_W - 1, D_CONV), F32)
    _, _, _, _, km, vm, um_last = _inproj(meta_tokens[None], zbuf, w_in_b, b_in_r, conv_w[0],
                                          gin, bin_, N_META)

    n_total = bp * tp + bs * ts

    def stream(x, cbuf, past_k, past_v, mk, mv, past_valid, tm_pref, row_off, prev):
        bsz, t, _ = x.shape
        q, ga, gc, oc, k, v, u_last = _inproj(x, cbuf, w_in_b, b_in_r, conv_w[0], gin, bin_,
                                             _pick_tile(t, tm_pref))
        tpad = -(-t // CHUNK) * CHUNK
        padq = ((0, 0), (0, tpad - t), (0, 0))
        kfull = jnp.concatenate([past_k, jnp.pad(k, padq)], axis=1).astype(BF16)
        vfull = jnp.concatenate([past_v, jnp.pad(v, padq)], axis=1).astype(BF16)
        oa = _attention(jnp.pad(q, padq), kfull, vfull, mk.astype(BF16), mv.astype(BF16), sinks,
                        past_valid, t, _pick_tile(tpad, tm_pref))[:, :t]
        n = bsz * t
        flat = lambda a: a.reshape(n, a.shape[-1])
        h1, idx, gate = _mix(flat(x), flat(oa), flat(oc), flat(ga), flat(gc), wa, wc, wo,
                                  gin, bin_, row2(ln1_g[0]), row2(ln1_b[0]), wr_t, br,
                                  _pick_tile(n, tm_pref), n_total, row_off, prev)
        return h1, idx, gate, k, v, u_last

    zk = jnp.zeros((bp, WINDOW, KV_W), F32)
    h1p, idxp, gatep, kp, vp, ulp = stream(x_prompt, um_last, zk, zk, km, vm, False, 512, 0, None)
    wk_s = cache_win_k[0].reshape(bs, WINDOW, KV_W)
    wv_s = cache_win_v[0].reshape(bs, WINDOW, KV_W)
    h1, idx, gate, ks, vs, uls = stream(
        x_sample, state_conv[0], wk_s, wv_s, cache_meta_k[0].reshape(bs, N_META, KV_W),
        cache_meta_v[0].reshape(bs, N_META, KV_W), True, 512, bp * tp, (h1p, idxp, gatep))

    n = n_total
    tmm = TOKEN_TILE
    ntile = n // tmm
    lpos, n8_tab = _plan(idx, tmm)
    t8 = n8_tab[:, :, 0].astype(jnp.int32)
    padded = (jnp.sum(t8, axis=0) + EXPERT_BLOCK - 1) // EXPERT_BLOCK * EXPERT_BLOCK
    pend = jnp.cumsum(padded)
    poff = pend - padded
    gstart = poff[None, :] + jnp.cumsum(t8, axis=0) - t8
    lstart = jnp.cumsum(t8, axis=1) - t8
    nb = -(-(n * TOP_K + ntile * N_EXPERTS * (SEG_ALIGN - 1)) // EXPERT_BLOCK) + N_EXPERTS
    block_e = jnp.minimum(
        jnp.sum((pend[None, :] <= (jnp.arange(nb) * EXPERT_BLOCK)[:, None]).astype(jnp.int32), axis=1),
        N_EXPERTS - 1).astype(jnp.int32)
    n_used = (pend[-1:] // EXPERT_BLOCK).astype(jnp.int32)
    last_blk = jnp.concatenate([jnp.where(padded > 0, pend - EXPERT_BLOCK, -1), pend[-1:]]).astype(jnp.int32)
    n8f, lof, gsf = t8.reshape(-1), lstart.reshape(-1).astype(jnp.int32), gstart.reshape(-1).astype(jnp.int32)

    xg = _dispatch(last_blk, n8f, lof, gsf, lpos, h1, nb * EXPERT_BLOCK, tmm)
    ybuf = _ffn(block_e, n_used, xg, wgu, bgu, wd, bd)
    y_p, y_s = _combine(n8f, lof, gsf, ybuf, lpos.T, gate.T, h1, row2(ln2_g[0]), row2(ln2_b[0]),
                        tmm, bp * tp)
    y_prompt = y_p.reshape(bp, tp, D_MODEL)
    y_sample = y_s.reshape(bs, ts, D_MODEL)
    kv5 = lambda a: a.reshape(a.shape[0], a.shape[1], N_KV, HEAD_DIM)[None]
    mk_p = jnp.broadcast_to(kv5(km), (1, bp, N_META, N_KV, HEAD_DIM))
    mv_p = jnp.broadcast_to(kv5(vm), (1, bp, N_META, N_KV, HEAD_DIM))
    wk_p = kv5(kp[:, -WINDOW:])
    wv_p = kv5(vp[:, -WINDOW:])
    wk_o = kv5(jnp.concatenate([wk_s, ks], axis=1)[:, -WINDOW:])
    wv_o = kv5(jnp.concatenate([wv_s, vs], axis=1)[:, -WINDOW:])
    return (y_prompt, y_sample, mk_p, mv_p, wk_p, wv_p, ulp[None], wk_o, wv_o, uls[None])
```

```python
import functools

import jax
import jax.numpy as jnp
from jax import lax
from jax.experimental import pallas as pl
from jax.experimental.pallas import tpu as pltpu

D_MODEL = 1024
CHUNK = 64
N_META = 16
N_Q = 16
N_KV = 4
GROUP = N_Q // N_KV
HEAD_DIM = 64
ATTN_W = N_Q * HEAD_DIM
KV_W = N_KV * HEAD_DIM
WINDOW = 128
D_CONV = D_MODEL
CONV_W = 3
N_EXPERTS = 32
TOP_K = 4
D_FF = D_MODEL
SWIGLU_LIMIT = 7.0
SWIGLU_ALPHA = 1.702
EXPERT_BLOCK = 512
TOKEN_TILE = 256
LN_EPS = 1e-5
DEPTH = 1
ALPHA = (2 * DEPTH) ** 0.25
ATTN_SCALE = HEAD_DIM ** -0.5

OFF_GA = ATTN_W
OFF_GC = OFF_GA + D_MODEL
OFF_CB = OFF_GC + D_MODEL
OFF_K = OFF_CB + D_CONV
OFF_V = OFF_K + KV_W
OFF_CC = OFF_V + KV_W
OFF_CH = OFF_CC + D_CONV
IN_COLS = OFF_CH + D_CONV

LANES = 128
VMEM_LIMIT = 56 * 1024 * 1024

F32 = jnp.float32
BF16 = jnp.bfloat16


def _ln(x, g, b):
    mu = jnp.mean(x, axis=-1, keepdims=True)
    xc = x - mu
    var = jnp.mean(xc * xc, axis=-1, keepdims=True)
    return (xc * lax.rsqrt(var + LN_EPS)) * g + b


def _const_spec(shape):
    nd = len(shape)
    return pl.BlockSpec(shape, lambda *_: (0,) * nd, pipeline_mode=pl.Buffered(1))


def _inproj_kernel(x_ref, cbuf_ref, w_ref, b_ref, cw_ref, g_ref, be_ref,
                   q_ref, ga_ref, gc_ref, oc_ref, k_ref, v_ref, kb_ref, vb_ref, ul_ref, carry_ref):
    tm = x_ref.shape[1]

    @pl.when(pl.program_id(1) == 0)
    def _():
        carry_ref[...] = cbuf_ref[0]

    h = _ln(x_ref[0], g_ref[...], be_ref[...]).astype(BF16)

    def proj(off, width):
        return (jnp.dot(h, w_ref[:, off:off + width], preferred_element_type=F32)
                + b_ref[:, off:off + width])

    q_ref[0] = (proj(0, ATTN_W) * ATTN_SCALE).astype(BF16)
    ga_ref[0] = jax.nn.sigmoid(proj(OFF_GA, D_MODEL)).astype(BF16)
    gc_ref[0] = jax.nn.sigmoid(proj(OFF_GC, D_MODEL)).astype(BF16)
    k = proj(OFF_K, KV_W)
    v = proj(OFF_V, KV_W)
    k_ref[0] = k
    v_ref[0] = v
    kb_ref[0] = k.astype(BF16)
    vb_ref[0] = v.astype(BF16)

    u = proj(OFF_CC, D_CONV) * proj(OFF_CH, D_CONV)
    prev = carry_ref[...]
    row = lax.broadcasted_iota(jnp.int32, (tm, D_CONV), 0)
    u1 = jnp.where(row == 0, prev[1:2], pltpu.roll(u, 1, 0))
    u2 = jnp.where(row == 0, prev[0:1],
                   jnp.where(row == 1, prev[1:2], pltpu.roll(u, 2, 0)))
    conv = cw_ref[0:1] * u2 + cw_ref[1:2] * u1 + cw_ref[2:3] * u
    oc_ref[0] = (proj(OFF_CB, D_CONV) * conv).astype(BF16)
    last = u[tm - (CONV_W - 1):tm]
    carry_ref[...] = last
    ul_ref[0] = last


def _inproj(x, cbuf, w_in, b_in, conv_w, ln_g, ln_b, tm):
    bsz, t, _ = x.shape
    cb_map = (lambda b, i: (b, 0, 0)) if cbuf.shape[0] == bsz else (lambda b, i: (0, 0, 0))
    row_spec = lambda w: pl.BlockSpec((1, tm, w), lambda b, i: (b, i, 0))
    outs = (
        jax.ShapeDtypeStruct((bsz, t, ATTN_W), BF16),
        jax.ShapeDtypeStruct((bsz, t, D_MODEL), BF16),
        jax.ShapeDtypeStruct((bsz, t, D_MODEL), BF16),
        jax.ShapeDtypeStruct((bsz, t, D_CONV), BF16),
        jax.ShapeDtypeStruct((bsz, t, KV_W), F32),
        jax.ShapeDtypeStruct((bsz, t, KV_W), F32),
        jax.ShapeDtypeStruct((bsz, t, KV_W), BF16),
        jax.ShapeDtypeStruct((bsz, t, KV_W), BF16),
        jax.ShapeDtypeStruct((bsz, CONV_W - 1, D_CONV), F32),
    )
    return pl.pallas_call(
        _inproj_kernel,
        out_shape=outs,
        grid=(bsz, t // tm),
        in_specs=[
            row_spec(D_MODEL),
            pl.BlockSpec((1, CONV_W - 1, D_CONV), cb_map),
            _const_spec((D_MODEL, IN_COLS)),
            _const_spec((1, IN_COLS)),
            _const_spec((CONV_W, D_CONV)),
            _const_spec((1, D_MODEL)),
            _const_spec((1, D_MODEL)),
        ],
        out_specs=(
            row_spec(ATTN_W), row_spec(D_MODEL), row_spec(D_MODEL), row_spec(D_CONV),
            row_spec(KV_W), row_spec(KV_W), row_spec(KV_W), row_spec(KV_W),
            pl.BlockSpec((1, CONV_W - 1, D_CONV), lambda b, i: (b, 0, 0)),
        ),
        scratch_shapes=[pltpu.VMEM((CONV_W - 1, D_CONV), F32)],
        compiler_params=pltpu.CompilerParams(
            dimension_semantics=("arbitrary", "arbitrary"), vmem_limit_bytes=VMEM_LIMIT),
        name="inproj",
    )(x, cbuf, w_in, b_in, conv_w, ln_g, ln_b)


def _attn_kernel(sink_ref, q_ref, k_ref, v_ref, mk_ref, mv_ref, o_ref, *, past_valid, t_valid):
    tq = q_ref.shape[1]
    nchunk = tq // CHUNK
    nkeys = WINDOW + CHUNK
    unroll = 2 if nchunk % 2 == 0 else 1
    tile = pl.program_id(1)
    mk = mk_ref[0]
    mv = mv_ref[0]
    contract = (((1,), (1,)), ((), ()))
    qrow = lax.broadcasted_iota(jnp.int32, (GROUP * CHUNK, 1), 0)
    col = lax.broadcasted_iota(jnp.int32, (1, N_META + nkeys), 1)

    def one_chunk(ci):
        row0 = pl.multiple_of((tile * nchunk + ci) * CHUNK, CHUNK)
        qoff = pl.multiple_of(ci * CHUNK, CHUNK)
        kc = jnp.concatenate([mk, k_ref[0, pl.ds(row0, nkeys), :]], axis=0)
        vc = jnp.concatenate([mv, v_ref[0, pl.ds(row0, nkeys), :]], axis=0)
        qc = q_ref[0, pl.ds(qoff, CHUNK), :]
        kpos = row0 + col - N_META
        valid = kpos < WINDOW + t_valid
        if not past_valid:
            valid = jnp.logical_and(valid, kpos >= WINDOW)
        valid = jnp.logical_or(valid, col < N_META)
        for g in range(N_KV):
            heads = [qc[:, (g * GROUP + i) * HEAD_DIM:(g * GROUP + i + 1) * HEAD_DIM]
                     for i in range(GROUP)]
            qg = jnp.concatenate(heads, axis=0)
            ksl = slice(g * HEAD_DIM, (g + 1) * HEAD_DIM)
            s = lax.dot_general(qg, kc[:, ksl], contract, preferred_element_type=F32)
            s = jnp.where(valid, s, -jnp.inf)
            sink = jnp.full((GROUP * CHUNK, 1), sink_ref[g * GROUP], F32)
            for i in range(1, GROUP):
                sink = jnp.where(qrow >= i * CHUNK, sink_ref[g * GROUP + i], sink)
            m = jnp.maximum(jnp.max(s, axis=-1, keepdims=True), sink)
            p = jnp.exp(s - m)
            den = jnp.sum(p, axis=-1, keepdims=True) + jnp.exp(sink - m)
            o = jnp.dot((p / den).astype(BF16), vc[:, ksl], preferred_element_type=F32)
            for i in range(GROUP):
                h = g * GROUP + i
                o_ref[0, pl.ds(qoff, CHUNK),
                      h * HEAD_DIM:(h + 1) * HEAD_DIM] = o[i * CHUNK:(i + 1) * CHUNK].astype(BF16)

    def body(it, carry):
        for u in range(unroll):
            one_chunk(it * unroll + u)
        return carry

    lax.fori_loop(0, nchunk // unroll, body, 0)


def _attention(q, kfull, vfull, mk, mv, sinks, past_valid, t_valid, tq):
    bsz, t, _ = q.shape
    tk = kfull.shape[1]
    m_map = (lambda b, i: (b, 0, 0)) if mk.shape[0] == bsz else (lambda b, i: (0, 0, 0))
    return pl.pallas_call(
        functools.partial(_attn_kernel, past_valid=past_valid, t_valid=t_valid),
        out_shape=jax.ShapeDtypeStruct((bsz, t, ATTN_W), BF16),
        grid=(bsz, t // tq),
        in_specs=[
            pl.BlockSpec(memory_space=pltpu.SMEM),
            pl.BlockSpec((1, tq, ATTN_W), lambda b, i: (b, i, 0)),
            pl.BlockSpec((1, tk, KV_W), lambda b, i: (b, 0, 0)),
            pl.BlockSpec((1, tk, KV_W), lambda b, i: (b, 0, 0)),
            pl.BlockSpec((1, N_META, KV_W), m_map),
            pl.BlockSpec((1, N_META, KV_W), m_map),
        ],
        out_specs=pl.BlockSpec((1, tq, ATTN_W), lambda b, i: (b, i, 0)),
        compiler_params=pltpu.CompilerParams(
            dimension_semantics=("arbitrary", "arbitrary"), vmem_limit_bytes=VMEM_LIMIT),
        name="attn",
    )(sinks, q, kfull, vfull, mk, mv)


SEG_ALIGN = 8


def _tile_plan(idxs):
    tw = idxs[0].shape[1]
    e_iota = lax.broadcasted_iota(jnp.int32, (N_EXPERTS, tw), 0)
    sels = [e_iota == ix for ix in idxs]
    oh = sels[0].astype(F32)
    for k in range(1, TOP_K):
        oh = oh + sels[k].astype(F32)
    cnt = jnp.sum(oh, axis=1, keepdims=True)
    n8 = jnp.floor((cnt + (SEG_ALIGN - 1)) * (1.0 / SEG_ALIGN)) * SEG_ALIGN
    n8b = jnp.broadcast_to(n8, (N_EXPERTS, LANES))
    low = (lax.broadcasted_iota(jnp.int32, (N_EXPERTS, N_EXPERTS), 0)
           > lax.broadcasted_iota(jnp.int32, (N_EXPERTS, N_EXPERTS), 1)).astype(BF16)
    loff = jnp.dot(low, n8b.astype(BF16), preferred_element_type=F32)[:, 0:1]
    tri = (lax.broadcasted_iota(jnp.int32, (tw, tw), 0)
           < lax.broadcasted_iota(jnp.int32, (tw, tw), 1)).astype(BF16)
    pos = jnp.dot(oh.astype(BF16), tri, preferred_element_type=F32) + loff
    lpos = [jnp.sum(jnp.where(s, pos, 0.0), axis=0, keepdims=True) for s in sels]
    return jnp.concatenate(lpos, axis=0).astype(jnp.int32), n8b


def _mix_kernel(x_ref, oa_ref, oc_ref, ga_ref, gc_ref, wa_ref, wc_ref, wo_ref,
                gin_ref, bin_ref, g1_ref, b1_ref, wr_ref, br_ref,
                h1_ref, lpos_ref, gate_ref, n8_ref):
    tm = x_ref.shape[0]
    h0 = _ln(x_ref[...], gin_ref[...], bin_ref[...])
    a = jnp.dot(oa_ref[...], wa_ref[...], preferred_element_type=F32)
    c = jnp.dot(oc_ref[...], wc_ref[...], preferred_element_type=F32)
    mixed = ga_ref[...].astype(F32) * a + gc_ref[...].astype(F32) * c
    y = jnp.dot(mixed.astype(BF16), wo_ref[...], preferred_element_type=F32)
    h1 = _ln(ALPHA * h0 + y, g1_ref[...], b1_ref[...])
    h1_ref[...] = h1

    logits = lax.dot_general(wr_ref[...], h1.astype(BF16), (((1,), (1,)), ((), ())),
                             preferred_element_type=F32) + br_ref[...]
    e_iota = lax.broadcasted_iota(jnp.int32, (N_EXPERTS, tm), 0)
    vals, idxs = [], []
    for _ in range(TOP_K):
        mx = jnp.max(logits, axis=0, keepdims=True)
        ix = jnp.min(jnp.where(logits == mx, e_iota, N_EXPERTS), axis=0, keepdims=True)
        sel = e_iota == ix
        vals.append(mx)
        idxs.append(ix)
        logits = jnp.where(sel, -jnp.inf, logits)
    ev = [jnp.exp(v - vals[0]) for v in vals]
    den = ev[0] + ev[1] + ev[2] + ev[3]
    gate_ref[...] = jnp.concatenate([e / den for e in ev], axis=0)
    for j in range(tm // TOKEN_TILE):
        cols = slice(j * TOKEN_TILE, (j + 1) * TOKEN_TILE)
        lpos_ref[:, cols], n8_ref[j] = _tile_plan([ix[:, cols] for ix in idxs])


def _mix(x, oa, oc, ga, gc, wa, wc, wo, gin, bin_, g1, b1, wr_t, br, tm, n_total, row_off, prev):
    n = x.shape[0]
    off = row_off // tm
    outs = (
        jax.ShapeDtypeStruct((n_total, D_MODEL), F32),
        jax.ShapeDtypeStruct((TOP_K, n_total), jnp.int32),
        jax.ShapeDtypeStruct((TOP_K, n_total), F32),
        jax.ShapeDtypeStruct((n_total // TOKEN_TILE, N_EXPERTS, LANES), F32),
    )
    nprev = 0 if prev is None else len(prev)
    ntile = n // tm
    extra = 0 if prev is not None else -(-(n_total - n) // tm)
    row = lambda w: pl.BlockSpec((tm, w), lambda i: (jnp.minimum(i, ntile - 1), 0))

    def kern(*refs):
        refs = refs[nprev:]
        if extra == 0:
            _mix_kernel(*refs)
            return

        @pl.when(pl.program_id(0) < ntile)
        def _():
            _mix_kernel(*refs)

        @pl.when(pl.program_id(0) >= ntile)
        def _():
            for o_ref in refs[-4:]:
                o_ref[...] = jnp.zeros_like(o_ref)

    return pl.pallas_call(
        kern,
        out_shape=outs,
        grid=(ntile + extra,),
        in_specs=[pl.BlockSpec(memory_space=pl.ANY)] * nprev + [
            row(D_MODEL), row(ATTN_W), row(D_CONV), row(D_MODEL), row(D_MODEL),
            _const_spec((ATTN_W, D_MODEL)), _const_spec((D_CONV, D_MODEL)),
            _const_spec((D_MODEL, D_MODEL)),
            _const_spec((1, D_MODEL)), _const_spec((1, D_MODEL)),
            _const_spec((1, D_MODEL)), _const_spec((1, D_MODEL)),
            _const_spec((N_EXPERTS, D_MODEL)), _const_spec((N_EXPERTS, 1)),
        ],
        out_specs=(
            pl.BlockSpec((tm, D_MODEL), lambda i: (i + off, 0)),
            pl.BlockSpec((TOP_K, tm), lambda i: (0, i + off)),
            pl.BlockSpec((TOP_K, tm), lambda i: (0, i + off)),
            pl.BlockSpec((tm // TOKEN_TILE, N_EXPERTS, LANES), lambda i: (i + off, 0, 0)),
        ),
        input_output_aliases={j: j for j in range(nprev)},
        compiler_params=pltpu.CompilerParams(
            dimension_semantics=("arbitrary",), vmem_limit_bytes=VMEM_LIMIT),
        name="mix",
    )(*(prev or ()), x, oa, oc, ga, gc, wa, wc, wo, gin, bin_, g1, b1, wr_t, br)


def _segment_copies(n8_ref, lo_ref, gs_ref, t, local_of, global_of, sem, to_global):
    def run(action):
        for e in range(N_EXPERTS):
            n = n8_ref[t * N_EXPERTS + e]

            @pl.when(n > 0)
            def _():
                rows = pl.multiple_of(n, SEG_ALIGN)
                loc = local_of(pl.multiple_of(lo_ref[t * N_EXPERTS + e], SEG_ALIGN), rows)
                glo = global_of(pl.multiple_of(gs_ref[t * N_EXPERTS + e], SEG_ALIGN), rows)
                src, dst = (loc, glo) if to_global else (glo, loc)
                action(pltpu.make_async_copy(src, dst, sem))
    return run


def _dispatch_kernel(last_ref, n8_ref, lo_ref, gs_ref, lpos_ref, h_ref, xg_hbm, xs_ref, zero_ref,
                     zsem, sem):
    tm = h_ref.shape[0]
    nloc = xs_ref.shape[1]
    i = pl.program_id(0)
    slot = i % 2

    @pl.when(i == 0)
    def _():
        zero_ref[...] = jnp.zeros_like(zero_ref)

        def zero_block(row):
            blk = pl.multiple_of(row, EXPERT_BLOCK)
            z = pltpu.make_async_copy(zero_ref, xg_hbm.at[pl.ds(blk, EXPERT_BLOCK), :], zsem)
            z.start()
            z.wait()

        for e in range(N_EXPERTS):
            @pl.when(last_ref[e] >= 0)
            def _():
                zero_block(last_ref[e])

        def tail(b, c):
            zero_block(b * EXPERT_BLOCK)
            return c

        lax.fori_loop(last_ref[N_EXPERTS] // EXPERT_BLOCK, xg_hbm.shape[0] // EXPERT_BLOCK, tail, 0)

    s_iota = lax.broadcasted_iota(jnp.int32, (nloc, tm), 0)
    hit = s_iota == lpos_ref[0:1, :]
    for k in range(1, TOP_K):
        hit = jnp.logical_or(hit, s_iota == lpos_ref[k:k + 1, :])
    perm = jnp.where(hit, 1.0, 0.0).astype(BF16)
    xs_ref[slot] = jnp.dot(perm, h_ref[...].astype(BF16), preferred_element_type=F32)

    def copies(t, s):
        return _segment_copies(n8_ref, lo_ref, gs_ref, t,
                               lambda r, n: xs_ref.at[s, pl.ds(r, n), :],
                               lambda r, n: xg_hbm.at[pl.ds(r, n), :], sem.at[s], True)

    copies(i, slot)(lambda d: d.start())

    @pl.when(i > 0)
    def _():
        copies(i - 1, 1 - slot)(lambda d: d.wait())

    @pl.when(i == pl.num_programs(0) - 1)
    def _():
        copies(i, slot)(lambda d: d.wait())


def _dispatch(last_blk, n8f, lof, gsf, lpos, h1, nrows, tm):
    ntile = h1.shape[0] // tm
    nloc = TOP_K * tm + N_EXPERTS * (SEG_ALIGN - 1)
    nloc = -(-nloc // SEG_ALIGN) * SEG_ALIGN
    grid_spec = pltpu.PrefetchScalarGridSpec(
        num_scalar_prefetch=4,
        grid=(ntile,),
        in_specs=[pl.BlockSpec((TOP_K, tm), lambda i, *_: (0, i)),
                  pl.BlockSpec((tm, D_MODEL), lambda i, *_: (i, 0))],
        out_specs=pl.BlockSpec(memory_space=pl.ANY),
        scratch_shapes=[pltpu.VMEM((2, nloc, D_MODEL), F32),
                        pltpu.VMEM((EXPERT_BLOCK, D_MODEL), F32),
                        pltpu.SemaphoreType.DMA, pltpu.SemaphoreType.DMA((2,))],
    )
    return pl.pallas_call(
        _dispatch_kernel,
        out_shape=jax.ShapeDtypeStruct((nrows, D_MODEL), F32),
        grid_spec=grid_spec,
        compiler_params=pltpu.CompilerParams(
            dimension_semantics=("arbitrary",), vmem_limit_bytes=VMEM_LIMIT),
        name="dispatch",
    )(last_blk, n8f, lof, gsf, lpos, h1)


GU_GROUP = 2 * LANES


def _wprep_kernel(w_ref, o_ref):
    r = lax.broadcasted_iota(jnp.int32, (GU_GROUP, GU_GROUP), 0)
    c = lax.broadcasted_iota(jnp.int32, (GU_GROUP, GU_GROUP), 1)
    perm = (r == jnp.where(c < LANES, 2 * c, 2 * (c - LANES) + 1)).astype(BF16)
    for m in range(2 * D_FF // GU_GROUP):
        cols = slice(m * GU_GROUP, (m + 1) * GU_GROUP)
        o_ref[0, :, cols] = jnp.dot(w_ref[0, :, cols].astype(BF16), perm,
                                    preferred_element_type=F32).astype(BF16)


def _wprep(w_gu):
    ne = w_gu.shape[0]
    return pl.pallas_call(
        _wprep_kernel,
        out_shape=jax.ShapeDtypeStruct(w_gu.shape, BF16),
        grid=(ne,),
        in_specs=[pl.BlockSpec((1, D_MODEL, 2 * D_FF), lambda e: (e, 0, 0))],
        out_specs=pl.BlockSpec((1, D_MODEL, 2 * D_FF), lambda e: (e, 0, 0)),
        compiler_params=pltpu.CompilerParams(
            dimension_semantics=("arbitrary",), vmem_limit_bytes=VMEM_LIMIT),
        name="wprep",
    )(w_gu)


def _ffn_kernel(be_ref, nu_ref, x_ref, wgu_ref, bgu_ref, wd_ref, bd_ref, y_ref):
    del be_ref

    @pl.when(pl.program_id(0) < nu_ref[0])
    def _():
        hgu = jnp.dot(x_ref[...].astype(BF16), wgu_ref[0], preferred_element_type=F32) + bgu_ref[0]
        ngrp = 2 * D_FF // GU_GROUP
        g = jnp.concatenate([hgu[:, m * GU_GROUP:m * GU_GROUP + LANES] for m in range(ngrp)], axis=-1)
        up = jnp.concatenate([hgu[:, m * GU_GROUP + LANES:(m + 1) * GU_GROUP] for m in range(ngrp)],
                             axis=-1)
        g = jnp.minimum(g, SWIGLU_LIMIT)
        up = jnp.clip(up, -SWIGLU_LIMIT, SWIGLU_LIMIT)
        act = (up + 1.0) * (g * jax.nn.sigmoid(g * SWIGLU_ALPHA))
        y_ref[...] = jnp.dot(act.astype(BF16), wd_ref[0], preferred_element_type=F32) + bd_ref[0]

    @pl.when(pl.program_id(0) >= nu_ref[0])
    def _():
        y_ref[...] = jnp.zeros_like(y_ref)


def _ffn(block_e, n_used, xg, wgu, bgu, wd, bd):
    nb = xg.shape[0] // EXPERT_BLOCK
    blk = lambda i, be, nu: (jnp.minimum(i, nu[0] - 1), 0)
    wsel = lambda i, be, nu: (be[jnp.minimum(i, nu[0] - 1)], 0, 0)
    grid_spec = pltpu.PrefetchScalarGridSpec(
        num_scalar_prefetch=2,
        grid=(nb,),
        in_specs=[
            pl.BlockSpec((EXPERT_BLOCK, D_MODEL), blk),
            pl.BlockSpec((1, D_MODEL, 2 * D_FF), wsel),
            pl.BlockSpec((1, 1, 2 * D_FF), wsel),
            pl.BlockSpec((1, D_FF, D_MODEL), wsel),
            pl.BlockSpec((1, 1, D_MODEL), wsel),
        ],
        out_specs=pl.BlockSpec((EXPERT_BLOCK, D_MODEL), lambda i, be, nu: (i, 0)),
    )
    return pl.pallas_call(
        _ffn_kernel,
        out_shape=jax.ShapeDtypeStruct(xg.shape, F32),
        grid_spec=grid_spec,
        compiler_params=pltpu.CompilerParams(
            dimension_semantics=("arbitrary",), vmem_limit_bytes=VMEM_LIMIT),
        name="ffn",
    )(block_e, n_used, xg, wgu, bgu, wd, bd)


def _combine_kernel(n8_ref, lo_ref, gs_ref, yb_hbm, lpos_ref, gate_ref, h1_ref, g2_ref, b2_ref,
                    outa_ref, outb_ref, ys_ref, sem, *, tiles_a):
    tm = h1_ref.shape[0]
    nloc = ys_ref.shape[1]
    i = pl.program_id(0)
    slot = i % 2

    def copies(t, s):
        return _segment_copies(n8_ref, lo_ref, gs_ref, t,
                               lambda r, n: ys_ref.at[s, pl.ds(r, n), :],
                               lambda r, n: yb_hbm.at[pl.ds(r, n), :], sem.at[s], False)

    @pl.when(i == 0)
    def _():
        ys_ref[...] = jnp.zeros_like(ys_ref)
        copies(0, 0)(lambda d: d.start())

    @pl.when(i + 1 < pl.num_programs(0))
    def _():
        copies(i + 1, 1 - slot)(lambda d: d.start())

    copies(i, slot)(lambda d: d.wait())

    s_iota = lax.broadcasted_iota(jnp.int32, (tm, nloc), 1)
    w = jnp.zeros((tm, nloc), F32)
    for k in range(TOP_K):
        w = jnp.where(s_iota == lpos_ref[:, k:k + 1], gate_ref[:, k:k + 1], w)
    moe = jnp.dot(w.astype(BF16), ys_ref[slot].astype(BF16), preferred_element_type=F32)
    out = _ln(ALPHA * h1_ref[...] + moe, g2_ref[...], b2_ref[...])

    @pl.when(i < tiles_a)
    def _():
        outa_ref[...] = out

    @pl.when(i >= tiles_a)
    def _():
        outb_ref[...] = out


def _combine(n8f, lof, gsf, ybuf, lpos_t, gate_t, h1, g2, b2, tm, n_a):
    n = h1.shape[0]
    tiles_a = n_a // tm
    nloc = TOP_K * tm + N_EXPERTS * (SEG_ALIGN - 1)
    nloc = -(-nloc // SEG_ALIGN) * SEG_ALIGN
    grid_spec = pltpu.PrefetchScalarGridSpec(
        num_scalar_prefetch=3,
        grid=(n // tm,),
        in_specs=[
            pl.BlockSpec(memory_space=pl.ANY),
            pl.BlockSpec((tm, TOP_K), lambda i, *_: (i, 0)),
            pl.BlockSpec((tm, TOP_K), lambda i, *_: (i, 0)),
            pl.BlockSpec((tm, D_MODEL), lambda i, *_: (i, 0)),
            pl.BlockSpec((1, D_MODEL), lambda i, *_: (0, 0)),
            pl.BlockSpec((1, D_MODEL), lambda i, *_: (0, 0)),
        ],
        out_specs=(pl.BlockSpec((tm, D_MODEL), lambda i, *_: (jnp.minimum(i, tiles_a - 1), 0)),
                   pl.BlockSpec((tm, D_MODEL), lambda i, *_: (jnp.maximum(i - tiles_a, 0), 0))),
        scratch_shapes=[pltpu.VMEM((2, nloc, D_MODEL), F32), pltpu.SemaphoreType.DMA((2,))],
    )
    return pl.pallas_call(
        functools.partial(_combine_kernel, tiles_a=tiles_a),
        out_shape=(jax.ShapeDtypeStruct((n_a, D_MODEL), F32),
                   jax.ShapeDtypeStruct((n - n_a, D_MODEL), F32)),
        grid_spec=grid_spec,
        compiler_params=pltpu.CompilerParams(
            dimension_semantics=("arbitrary",), vmem_limit_bytes=VMEM_LIMIT),
        name="combine",
    )(n8f, lof, gsf, ybuf, lpos_t, gate_t, h1, g2, b2)


def _pick_tile(t, pref):
    return pref if t % pref == 0 else t


def kernel(x_prompt, x_sample, cache_meta_k, cache_meta_v, cache_win_k, cache_win_v, state_conv,
           meta_tokens, ln_in_g, ln_in_b, w_in, b_in, conv_w, attn_sinks, w_attn_br, w_conv_br, w_o,
           ln1_g, ln1_b, w_router, b_router, w_gu, b_gu, w_d, b_d, ln2_g, ln2_b):
    bp, tp, _ = x_prompt.shape
    bs, ts, _ = x_sample.shape
    row2 = lambda a: a.reshape(1, -1)

    w_in_b = w_in[0].astype(BF16)
    b_in_r = row2(b_in[0])
    gin, bin_ = row2(ln_in_g), row2(ln_in_b)
    wa, wc, wo = w_attn_br[0].astype(BF16), w_conv_br[0].astype(BF16), w_o[0].astype(BF16)
    wr_t = w_router[0].T.astype(BF16)
    br = b_router[0].reshape(N_EXPERTS, 1)
    wgu = _wprep(w_gu[0])
    bgu = (b_gu[0].reshape(N_EXPERTS, 2 * D_FF // GU_GROUP, LANES, 2).transpose(0, 1, 3, 2)
           .reshape(N_EXPERTS, 1, 2 * D_FF))
    wd = w_d[0].astype(BF16)
    bd = b_d[0][:, None, :]
    sinks = attn_sinks[0]

    zbuf = jnp.zeros((1, CONV_W - 1, D_CONV), F32)
    _, _, _, _, km, vm, _, _, um_last = _inproj(meta_tokens[None], zbuf, w_in_b, b_in_r, conv_w[0],
                                          gin, bin_, N_META)

    n_total = bp * tp + bs * ts

    def stream(x, cbuf, past_k, past_v, mk, mv, past_valid, tm_pref, row_off, prev):
        bsz, t, _ = x.shape
        q, ga, gc, oc, k, v, kb, vb, u_last = _inproj(x, cbuf, w_in_b, b_in_r, conv_w[0], gin, bin_,
                                             _pick_tile(t, tm_pref))
        tpad = -(-t // CHUNK) * CHUNK
        padq = ((0, 0), (0, tpad - t), (0, 0))
        kfull = jnp.concatenate([past_k.astype(BF16), jnp.pad(kb, padq)], axis=1)
        vfull = jnp.concatenate([past_v.astype(BF16), jnp.pad(vb, padq)], axis=1)
        oa = _attention(jnp.pad(q, padq), kfull, vfull, mk.astype(BF16), mv.astype(BF16), sinks,
                        past_valid, t, _pick_tile(tpad, tm_pref))[:, :t]
        n = bsz * t
        flat = lambda a: a.reshape(n, a.shape[-1])
        h1, lpos, gate, n8_tab = _mix(flat(x), flat(oa), flat(oc), flat(ga), flat(gc), wa, wc, wo,
                                  gin, bin_, row2(ln1_g[0]), row2(ln1_b[0]), wr_t, br,
                                  _pick_tile(n, tm_pref), n_total, row_off, prev)
        return h1, lpos, gate, n8_tab, k, v, u_last

    zk = jnp.zeros((bp, WINDOW, KV_W), F32)
    h1p, lposp, gatep, n8p, kp, vp, ulp = stream(x_prompt, um_last, zk, zk, km, vm, False, 512, 0, None)
    wk_s = cache_win_k[0].reshape(bs, WINDOW, KV_W)
    wv_s = cache_win_v[0].reshape(bs, WINDOW, KV_W)
    h1, lpos, gate, n8_tab, ks, vs, uls = stream(
        x_sample, state_conv[0], wk_s, wv_s, cache_meta_k[0].reshape(bs, N_META, KV_W),
        cache_meta_v[0].reshape(bs, N_META, KV_W), True, 512, bp * tp, (h1p, lposp, gatep, n8p))

    n = n_total
    tmm = TOKEN_TILE
    ntile = n // tmm
    t8 = n8_tab[:, :, 0].astype(jnp.int32)
    padded = (jnp.sum(t8, axis=0) + EXPERT_BLOCK - 1) // EXPERT_BLOCK * EXPERT_BLOCK
    pend = jnp.cumsum(padded)
    poff = pend - padded
    gstart = poff[None, :] + jnp.cumsum(t8, axis=0) - t8
    lstart = jnp.cumsum(t8, axis=1) - t8
    nb = -(-(n * TOP_K + ntile * N_EXPERTS * (SEG_ALIGN - 1)) // EXPERT_BLOCK) + N_EXPERTS
    block_e = jnp.minimum(
        jnp.sum((pend[None, :] <= (jnp.arange(nb) * EXPERT_BLOCK)[:, None]).astype(jnp.int32), axis=1),
        N_EXPERTS - 1).astype(jnp.int32)
    n_used = (pend[-1:] // EXPERT_BLOCK).astype(jnp.int32)
    last_blk = jnp.concatenate([jnp.where(padded > 0, pend - EXPERT_BLOCK, -1), pend[-1:]]).astype(jnp.int32)
    n8f, lof, gsf = t8.reshape(-1), lstart.reshape(-1).astype(jnp.int32), gstart.reshape(-1).astype(jnp.int32)

    xg = _dispatch(last_blk, n8f, lof, gsf, lpos, h1, nb * EXPERT_BLOCK, tmm)
    ybuf = _ffn(block_e, n_used, xg, wgu, bgu, wd, bd)
    y_p, y_s = _combine(n8f, lof, gsf, ybuf, lpos.T, gate.T, h1, row2(ln2_g[0]), row2(ln2_b[0]),
                        tmm, bp * tp)
    y_prompt = y_p.reshape(bp, tp, D_MODEL)
    y_sample = y_s.reshape(bs, ts, D_MODEL)
    kv5 = lambda a: a.reshape(a.shape[0], a.shape[1], N_KV, HEAD_DIM)[None]
    mk_p = jnp.broadcast_to(kv5(km), (1, bp, N_META, N_KV, HEAD_DIM))
    mv_p = jnp.broadcast_to(kv5(vm), (1, bp, N_META, N_KV, HEAD_DIM))
    wk_p = kv5(kp[:, -WINDOW:])
    wv_p = kv5(vp[:, -WINDOW:])
    wk_o = kv5(jnp.concatenate([wk_s, ks], axis=1)[:, -WINDOW:])
    wv_o = kv5(jnp.concatenate([wv_s, vs], axis=1)[:, -WINDOW:])
    return (y_prompt, y_sample, mk_p, mv_p, wk_p, wv_p, ulp[None], wk_o, wv_o, uls[None])
```

```python
import functools

import jax
import jax.numpy as jnp
from jax import lax
from jax.experimental import pallas as pl
from jax.experimental.pallas import tpu as pltpu

D_MODEL = 1024
CHUNK = 64
N_META = 16
N_Q = 16
N_KV = 4
GROUP = N_Q // N_KV
HEAD_DIM = 64
ATTN_W = N_Q * HEAD_DIM
KV_W = N_KV * HEAD_DIM
WINDOW = 128
D_CONV = D_MODEL
CONV_W = 3
N_EXPERTS = 32
TOP_K = 4
D_FF = D_MODEL
SWIGLU_LIMIT = 7.0
SWIGLU_ALPHA = 1.702
EXPERT_BLOCK = 1024
TOKEN_TILE = 256
LN_EPS = 1e-5
DEPTH = 1
ALPHA = (2 * DEPTH) ** 0.25
ATTN_SCALE = HEAD_DIM ** -0.5

OFF_GA = ATTN_W
OFF_GC = OFF_GA + D_MODEL
OFF_CB = OFF_GC + D_MODEL
OFF_K = OFF_CB + D_CONV
OFF_V = OFF_K + KV_W
OFF_CC = OFF_V + KV_W
OFF_CH = OFF_CC + D_CONV
IN_COLS = OFF_CH + D_CONV

LANES = 128
VMEM_LIMIT = 56 * 1024 * 1024

F32 = jnp.float32
BF16 = jnp.bfloat16


def _ln(x, g, b):
    mu = jnp.mean(x, axis=-1, keepdims=True)
    xc = x - mu
    var = jnp.mean(xc * xc, axis=-1, keepdims=True)
    return (xc * lax.rsqrt(var + LN_EPS)) * g + b


def _const_spec(shape):
    nd = len(shape)
    return pl.BlockSpec(shape, lambda *_: (0,) * nd, pipeline_mode=pl.Buffered(1))


def _inproj_kernel(x_ref, cbuf_ref, w_ref, b_ref, cw_ref, g_ref, be_ref,
                   q_ref, ga_ref, gc_ref, oc_ref, k_ref, v_ref, kb_ref, vb_ref, ul_ref, carry_ref):
    tm = x_ref.shape[1]

    @pl.when(pl.program_id(1) == 0)
    def _():
        carry_ref[...] = cbuf_ref[0]

    h = _ln(x_ref[0], g_ref[...], be_ref[...]).astype(BF16)

    def proj(off, width):
        return (jnp.dot(h, w_ref[:, off:off + width], preferred_element_type=F32)
                + b_ref[:, off:off + width])

    q_ref[0] = (proj(0, ATTN_W) * ATTN_SCALE).astype(BF16)
    ga_ref[0] = jax.nn.sigmoid(proj(OFF_GA, D_MODEL)).astype(BF16)
    gc_ref[0] = jax.nn.sigmoid(proj(OFF_GC, D_MODEL)).astype(BF16)
    k = proj(OFF_K, KV_W)
    v = proj(OFF_V, KV_W)
    k_ref[0] = k
    v_ref[0] = v
    kb_ref[0] = k.astype(BF16)
    vb_ref[0] = v.astype(BF16)

    u = proj(OFF_CC, D_CONV) * proj(OFF_CH, D_CONV)
    prev = carry_ref[...]
    row = lax.broadcasted_iota(jnp.int32, (tm, D_CONV), 0)
    u1 = jnp.where(row == 0, prev[1:2], pltpu.roll(u, 1, 0))
    u2 = jnp.where(row == 0, prev[0:1],
                   jnp.where(row == 1, prev[1:2], pltpu.roll(u, 2, 0)))
    conv = cw_ref[0:1] * u2 + cw_ref[1:2] * u1 + cw_ref[2:3] * u
    oc_ref[0] = (proj(OFF_CB, D_CONV) * conv).astype(BF16)
    last = u[tm - (CONV_W - 1):tm]
    carry_ref[...] = last
    ul_ref[0] = last


def _inproj(x, cbuf, w_in, b_in, conv_w, ln_g, ln_b, tm):
    bsz, t, _ = x.shape
    cb_map = (lambda b, i: (b, 0, 0)) if cbuf.shape[0] == bsz else (lambda b, i: (0, 0, 0))
    row_spec = lambda w: pl.BlockSpec((1, tm, w), lambda b, i: (b, i, 0))
    outs = (
        jax.ShapeDtypeStruct((bsz, t, ATTN_W), BF16),
        jax.ShapeDtypeStruct((bsz, t, D_MODEL), BF16),
        jax.ShapeDtypeStruct((bsz, t, D_MODEL), BF16),
        jax.ShapeDtypeStruct((bsz, t, D_CONV), BF16),
        jax.ShapeDtypeStruct((bsz, t, KV_W), F32),
        jax.ShapeDtypeStruct((bsz, t, KV_W), F32),
        jax.ShapeDtypeStruct((bsz, t, KV_W), BF16),
        jax.ShapeDtypeStruct((bsz, t, KV_W), BF16),
        jax.ShapeDtypeStruct((bsz, CONV_W - 1, D_CONV), F32),
    )
    return pl.pallas_call(
        _inproj_kernel,
        out_shape=outs,
        grid=(bsz, t // tm),
        in_specs=[
            row_spec(D_MODEL),
            pl.BlockSpec((1, CONV_W - 1, D_CONV), cb_map),
            _const_spec((D_MODEL, IN_COLS)),
            _const_spec((1, IN_COLS)),
            _const_spec((CONV_W, D_CONV)),
            _const_spec((1, D_MODEL)),
            _const_spec((1, D_MODEL)),
        ],
        out_specs=(
            row_spec(ATTN_W), row_spec(D_MODEL), row_spec(D_MODEL), row_spec(D_CONV),
            row_spec(KV_W), row_spec(KV_W), row_spec(KV_W), row_spec(KV_W),
            pl.BlockSpec((1, CONV_W - 1, D_CONV), lambda b, i: (b, 0, 0)),
        ),
        scratch_shapes=[pltpu.VMEM((CONV_W - 1, D_CONV), F32)],
        compiler_params=pltpu.CompilerParams(
            dimension_semantics=("arbitrary", "arbitrary"), vmem_limit_bytes=VMEM_LIMIT),
        name="inproj",
    )(x, cbuf, w_in, b_in, conv_w, ln_g, ln_b)


def _attn_kernel(sink_ref, q_ref, k_ref, v_ref, mk_ref, mv_ref, o_ref, *, past_valid, t_valid):
    tq = q_ref.shape[1]
    nchunk = tq // CHUNK
    nkeys = WINDOW + CHUNK
    unroll = 2 if nchunk % 2 == 0 else 1
    tile = pl.program_id(1)
    mk = mk_ref[0]
    mv = mv_ref[0]
    contract = (((1,), (1,)), ((), ()))
    qrow = lax.broadcasted_iota(jnp.int32, (GROUP * CHUNK, 1), 0)
    col = lax.broadcasted_iota(jnp.int32, (1, N_META + nkeys), 1)

    def one_chunk(ci):
        row0 = pl.multiple_of((tile * nchunk + ci) * CHUNK, CHUNK)
        qoff = pl.multiple_of(ci * CHUNK, CHUNK)
        kc = jnp.concatenate([mk, k_ref[0, pl.ds(row0, nkeys), :]], axis=0)
        vc = jnp.concatenate([mv, v_ref[0, pl.ds(row0, nkeys), :]], axis=0)
        qc = q_ref[0, pl.ds(qoff, CHUNK), :]
        kpos = row0 + col - N_META
        valid = kpos < WINDOW + t_valid
        if not past_valid:
            valid = jnp.logical_and(valid, kpos >= WINDOW)
        valid = jnp.logical_or(valid, col < N_META)
        for g in range(N_KV):
            heads = [qc[:, (g * GROUP + i) * HEAD_DIM:(g * GROUP + i + 1) * HEAD_DIM]
                     for i in range(GROUP)]
            qg = jnp.concatenate(heads, axis=0)
            ksl = slice(g * HEAD_DIM, (g + 1) * HEAD_DIM)
            s = lax.dot_general(qg, kc[:, ksl], contract, preferred_element_type=F32)
            s = jnp.where(valid, s, -jnp.inf)
            sink = jnp.full((GROUP * CHUNK, 1), sink_ref[g * GROUP], F32)
            for i in range(1, GROUP):
                sink = jnp.where(qrow >= i * CHUNK, sink_ref[g * GROUP + i], sink)
            m = jnp.maximum(jnp.max(s, axis=-1, keepdims=True), sink)
            p = jnp.exp(s - m)
            den = jnp.sum(p, axis=-1, keepdims=True) + jnp.exp(sink - m)
            o = jnp.dot((p / den).astype(BF16), vc[:, ksl], preferred_element_type=F32)
            for i in range(GROUP):
                h = g * GROUP + i
                o_ref[0, pl.ds(qoff, CHUNK),
                      h * HEAD_DIM:(h + 1) * HEAD_DIM] = o[i * CHUNK:(i + 1) * CHUNK].astype(BF16)

    def body(it, carry):
        for u in range(unroll):
            one_chunk(it * unroll + u)
        return carry

    lax.fori_loop(0, nchunk // unroll, body, 0)


def _attention(q, kfull, vfull, mk, mv, sinks, past_valid, t_valid, tq):
    bsz, t, _ = q.shape
    tk = kfull.shape[1]
    m_map = (lambda b, i: (b, 0, 0)) if mk.shape[0] == bsz else (lambda b, i: (0, 0, 0))
    return pl.pallas_call(
        functools.partial(_attn_kernel, past_valid=past_valid, t_valid=t_valid),
        out_shape=jax.ShapeDtypeStruct((bsz, t, ATTN_W), BF16),
        grid=(bsz, t // tq),
        in_specs=[
            pl.BlockSpec(memory_space=pltpu.SMEM),
            pl.BlockSpec((1, tq, ATTN_W), lambda b, i: (b, i, 0)),
            pl.BlockSpec((1, tk, KV_W), lambda b, i: (b, 0, 0)),
            pl.BlockSpec((1, tk, KV_W), lambda b, i: (b, 0, 0)),
            pl.BlockSpec((1, N_META, KV_W), m_map),
            pl.BlockSpec((1, N_META, KV_W), m_map),
        ],
        out_specs=pl.BlockSpec((1, tq, ATTN_W), lambda b, i: (b, i, 0)),
        compiler_params=pltpu.CompilerParams(
            dimension_semantics=("arbitrary", "arbitrary"), vmem_limit_bytes=VMEM_LIMIT),
        name="attn",
    )(sinks, q, kfull, vfull, mk, mv)


SEG_ALIGN = 8


def _tile_plan(idxs):
    tw = idxs[0].shape[1]
    e_iota = lax.broadcasted_iota(jnp.int32, (N_EXPERTS, tw), 0)
    sels = [e_iota == ix for ix in idxs]
    oh = sels[0].astype(F32)
    for k in range(1, TOP_K):
        oh = oh + sels[k].astype(F32)
    cnt = jnp.sum(oh, axis=1, keepdims=True)
    n8 = jnp.floor((cnt + (SEG_ALIGN - 1)) * (1.0 / SEG_ALIGN)) * SEG_ALIGN
    n8b = jnp.broadcast_to(n8, (N_EXPERTS, LANES))
    low = (lax.broadcasted_iota(jnp.int32, (N_EXPERTS, N_EXPERTS), 0)
           > lax.broadcasted_iota(jnp.int32, (N_EXPERTS, N_EXPERTS), 1)).astype(BF16)
    loff = jnp.dot(low, n8b.astype(BF16), preferred_element_type=F32)[:, 0:1]
    tri = (lax.broadcasted_iota(jnp.int32, (tw, tw), 0)
           < lax.broadcasted_iota(jnp.int32, (tw, tw), 1)).astype(BF16)
    pos = jnp.dot(oh.astype(BF16), tri, preferred_element_type=F32) + loff
    lpos = [jnp.sum(jnp.where(s, pos, 0.0), axis=0, keepdims=True) for s in sels]
    return jnp.concatenate(lpos, axis=0).astype(jnp.int32), n8b


def _mix_kernel(x_ref, oa_ref, oc_ref, ga_ref, gc_ref, wa_ref, wc_ref, wo_ref,
                gin_ref, bin_ref, g1_ref, b1_ref, wr_ref, br_ref,
                h1_ref, lpos_ref, gate_ref, n8_ref):
    tm = x_ref.shape[0]
    h0 = _ln(x_ref[...], gin_ref[...], bin_ref[...])
    a = jnp.dot(oa_ref[...], wa_ref[...], preferred_element_type=F32)
    c = jnp.dot(oc_ref[...], wc_ref[...], preferred_element_type=F32)
    mixed = ga_ref[...].astype(F32) * a + gc_ref[...].astype(F32) * c
    y = jnp.dot(mixed.astype(BF16), wo_ref[...], preferred_element_type=F32)
    h1 = _ln(ALPHA * h0 + y, g1_ref[...], b1_ref[...])
    h1_ref[...] = h1

    logits = lax.dot_general(wr_ref[...], h1.astype(BF16), (((1,), (1,)), ((), ())),
                             preferred_element_type=F32) + br_ref[...]
    e_iota = lax.broadcasted_iota(jnp.int32, (N_EXPERTS, tm), 0)
    vals, idxs = [], []
    for _ in range(TOP_K):
        mx = jnp.max(logits, axis=0, keepdims=True)
        ix = jnp.min(jnp.where(logits == mx, e_iota, N_EXPERTS), axis=0, keepdims=True)
        sel = e_iota == ix
        vals.append(mx)
        idxs.append(ix)
        logits = jnp.where(sel, -jnp.inf, logits)
    ev = [jnp.exp(v - vals[0]) for v in vals]
    den = ev[0] + ev[1] + ev[2] + ev[3]
    gate_ref[...] = jnp.concatenate([e / den for e in ev], axis=0)
    for j in range(tm // TOKEN_TILE):
        cols = slice(j * TOKEN_TILE, (j + 1) * TOKEN_TILE)
        lpos_ref[:, cols], n8_ref[j] = _tile_plan([ix[:, cols] for ix in idxs])


def _mix(x, oa, oc, ga, gc, wa, wc, wo, gin, bin_, g1, b1, wr_t, br, tm, n_total, row_off, prev):
    n = x.shape[0]
    off = row_off // tm
    outs = (
        jax.ShapeDtypeStruct((n_total, D_MODEL), F32),
        jax.ShapeDtypeStruct((TOP_K, n_total), jnp.int32),
        jax.ShapeDtypeStruct((TOP_K, n_total), F32),
        jax.ShapeDtypeStruct((n_total // TOKEN_TILE, N_EXPERTS, LANES), F32),
    )
    nprev = 0 if prev is None else len(prev)
    ntile = n // tm
    extra = 0 if prev is not None else -(-(n_total - n) // tm)
    row = lambda w: pl.BlockSpec((tm, w), lambda i: (jnp.minimum(i, ntile - 1), 0))

    def kern(*refs):
        refs = refs[nprev:]
        if extra == 0:
            _mix_kernel(*refs)
            return

        @pl.when(pl.program_id(0) < ntile)
        def _():
            _mix_kernel(*refs)

        @pl.when(pl.program_id(0) >= ntile)
        def _():
            for o_ref in refs[-4:]:
                o_ref[...] = jnp.zeros_like(o_ref)

    return pl.pallas_call(
        kern,
        out_shape=outs,
        grid=(ntile + extra,),
        in_specs=[pl.BlockSpec(memory_space=pl.ANY)] * nprev + [
            row(D_MODEL), row(ATTN_W), row(D_CONV), row(D_MODEL), row(D_MODEL),
            _const_spec((ATTN_W, D_MODEL)), _const_spec((D_CONV, D_MODEL)),
            _const_spec((D_MODEL, D_MODEL)),
            _const_spec((1, D_MODEL)), _const_spec((1, D_MODEL)),
            _const_spec((1, D_MODEL)), _const_spec((1, D_MODEL)),
            _const_spec((N_EXPERTS, D_MODEL)), _const_spec((N_EXPERTS, 1)),
        ],
        out_specs=(
            pl.BlockSpec((tm, D_MODEL), lambda i: (i + off, 0)),
            pl.BlockSpec((TOP_K, tm), lambda i: (0, i + off)),
            pl.BlockSpec((TOP_K, tm), lambda i: (0, i + off)),
            pl.BlockSpec((tm // TOKEN_TILE, N_EXPERTS, LANES), lambda i: (i + off, 0, 0)),
        ),
        input_output_aliases={j: j for j in range(nprev)},
        compiler_params=pltpu.CompilerParams(
            dimension_semantics=("arbitrary",), vmem_limit_bytes=VMEM_LIMIT),
        name="mix",
    )(*(prev or ()), x, oa, oc, ga, gc, wa, wc, wo, gin, bin_, g1, b1, wr_t, br)


def _segment_copies(n8_ref, lo_ref, gs_ref, t, local_of, global_of, sem, to_global):
    def run(action):
        for e in range(N_EXPERTS):
            n = n8_ref[t * N_EXPERTS + e]

            @pl.when(n > 0)
            def _():
                rows = pl.multiple_of(n, SEG_ALIGN)
                loc = local_of(pl.multiple_of(lo_ref[t * N_EXPERTS + e], SEG_ALIGN), rows)
                glo = global_of(pl.multiple_of(gs_ref[t * N_EXPERTS + e], SEG_ALIGN), rows)
                src, dst = (loc, glo) if to_global else (glo, loc)
                action(pltpu.make_async_copy(src, dst, sem))
    return run


def _dispatch_kernel(last_ref, n8_ref, lo_ref, gs_ref, lpos_ref, h_ref, xg_hbm, xs_ref, zero_ref,
                     zsem, sem):
    tm = h_ref.shape[0]
    nloc = xs_ref.shape[1]
    i = pl.program_id(0)
    slot = i % 2

    @pl.when(i == 0)
    def _():
        zero_ref[...] = jnp.zeros_like(zero_ref)

        def zero_block(row):
            blk = pl.multiple_of(row, EXPERT_BLOCK)
            z = pltpu.make_async_copy(zero_ref, xg_hbm.at[pl.ds(blk, EXPERT_BLOCK), :], zsem)
            z.start()
            z.wait()

        for e in range(N_EXPERTS):
            @pl.when(last_ref[e] >= 0)
            def _():
                zero_block(last_ref[e])

        def tail(b, c):
            zero_block(b * EXPERT_BLOCK)
            return c

        lax.fori_loop(last_ref[N_EXPERTS] // EXPERT_BLOCK, xg_hbm.shape[0] // EXPERT_BLOCK, tail, 0)

    s_iota = lax.broadcasted_iota(jnp.int32, (nloc, tm), 0)
    hit = s_iota == lpos_ref[0:1, :]
    for k in range(1, TOP_K):
        hit = jnp.logical_or(hit, s_iota == lpos_ref[k:k + 1, :])
    perm = jnp.where(hit, 1.0, 0.0).astype(BF16)
    xs_ref[slot] = jnp.dot(perm, h_ref[...].astype(BF16), preferred_element_type=F32)

    def copies(t, s):
        return _segment_copies(n8_ref, lo_ref, gs_ref, t,
                               lambda r, n: xs_ref.at[s, pl.ds(r, n), :],
                               lambda r, n: xg_hbm.at[pl.ds(r, n), :], sem.at[s], True)

    copies(i, slot)(lambda d: d.start())

    @pl.when(i > 0)
    def _():
        copies(i - 1, 1 - slot)(lambda d: d.wait())

    @pl.when(i == pl.num_programs(0) - 1)
    def _():
        copies(i, slot)(lambda d: d.wait())


def _dispatch(last_blk, n8f, lof, gsf, lpos, h1, nrows, tm):
    ntile = h1.shape[0] // tm
    nloc = TOP_K * tm + N_EXPERTS * (SEG_ALIGN - 1)
    nloc = -(-nloc // SEG_ALIGN) * SEG_ALIGN
    grid_spec = pltpu.PrefetchScalarGridSpec(
        num_scalar_prefetch=4,
        grid=(ntile,),
        in_specs=[pl.BlockSpec((TOP_K, tm), lambda i, *_: (0, i)),
                  pl.BlockSpec((tm, D_MODEL), lambda i, *_: (i, 0))],
        out_specs=pl.BlockSpec(memory_space=pl.ANY),
        scratch_shapes=[pltpu.VMEM((2, nloc, D_MODEL), F32),
                        pltpu.VMEM((EXPERT_BLOCK, D_MODEL), F32),
                        pltpu.SemaphoreType.DMA, pltpu.SemaphoreType.DMA((2,))],
    )
    return pl.pallas_call(
        _dispatch_kernel,
        out_shape=jax.ShapeDtypeStruct((nrows, D_MODEL), F32),
        grid_spec=grid_spec,
        compiler_params=pltpu.CompilerParams(
            dimension_semantics=("arbitrary",), vmem_limit_bytes=VMEM_LIMIT),
        name="dispatch",
    )(last_blk, n8f, lof, gsf, lpos, h1)


GU_GROUP = 2 * LANES


def _wprep_kernel(w_ref, o_ref):
    r = lax.broadcasted_iota(jnp.int32, (GU_GROUP, GU_GROUP), 0)
    c = lax.broadcasted_iota(jnp.int32, (GU_GROUP, GU_GROUP), 1)
    perm = (r == jnp.where(c < LANES, 2 * c, 2 * (c - LANES) + 1)).astype(BF16)
    for m in range(2 * D_FF // GU_GROUP):
        cols = slice(m * GU_GROUP, (m + 1) * GU_GROUP)
        o_ref[0, :, cols] = jnp.dot(w_ref[0, :, cols].astype(BF16), perm,
                                    preferred_element_type=F32).astype(BF16)


def _wprep(w_gu):
    ne = w_gu.shape[0]
    return pl.pallas_call(
        _wprep_kernel,
        out_shape=jax.ShapeDtypeStruct(w_gu.shape, BF16),
        grid=(ne,),
        in_specs=[pl.BlockSpec((1, D_MODEL, 2 * D_FF), lambda e: (e, 0, 0))],
        out_specs=pl.BlockSpec((1, D_MODEL, 2 * D_FF), lambda e: (e, 0, 0)),
        compiler_params=pltpu.CompilerParams(
            dimension_semantics=("arbitrary",), vmem_limit_bytes=VMEM_LIMIT),
        name="wprep",
    )(w_gu)


def _ffn_kernel(be_ref, nu_ref, x_ref, wgu_ref, bgu_ref, wd_ref, bd_ref, y_ref):
    del be_ref

    @pl.when(pl.program_id(0) < nu_ref[0])
    def _():
        hgu = jnp.dot(x_ref[...].astype(BF16), wgu_ref[0], preferred_element_type=F32) + bgu_ref[0]
        ngrp = 2 * D_FF // GU_GROUP
        g = jnp.concatenate([hgu[:, m * GU_GROUP:m * GU_GROUP + LANES] for m in range(ngrp)], axis=-1)
        up = jnp.concatenate([hgu[:, m * GU_GROUP + LANES:(m + 1) * GU_GROUP] for m in range(ngrp)],
                             axis=-1)
        g = jnp.minimum(g, SWIGLU_LIMIT)
        up = jnp.clip(up, -SWIGLU_LIMIT, SWIGLU_LIMIT)
        act = (up + 1.0) * (g * jax.nn.sigmoid(g * SWIGLU_ALPHA))
        y_ref[...] = jnp.dot(act.astype(BF16), wd_ref[0], preferred_element_type=F32) + bd_ref[0]

    @pl.when(pl.program_id(0) >= nu_ref[0])
    def _():
        y_ref[...] = jnp.zeros_like(y_ref)


def _ffn(block_e, n_used, xg, wgu, bgu, wd, bd):
    nb = xg.shape[0] // EXPERT_BLOCK
    blk = lambda i, be, nu: (jnp.minimum(i, nu[0] - 1), 0)
    wsel = lambda i, be, nu: (be[jnp.minimum(i, nu[0] - 1)], 0, 0)
    grid_spec = pltpu.PrefetchScalarGridSpec(
        num_scalar_prefetch=2,
        grid=(nb,),
        in_specs=[
            pl.BlockSpec((EXPERT_BLOCK, D_MODEL), blk),
            pl.BlockSpec((1, D_MODEL, 2 * D_FF), wsel),
            pl.BlockSpec((1, 1, 2 * D_FF), wsel),
            pl.BlockSpec((1, D_FF, D_MODEL), wsel),
            pl.BlockSpec((1, 1, D_MODEL), wsel),
        ],
        out_specs=pl.BlockSpec((EXPERT_BLOCK, D_MODEL), lambda i, be, nu: (i, 0)),
    )
    return pl.pallas_call(
        _ffn_kernel,
        out_shape=jax.ShapeDtypeStruct(xg.shape, F32),
        grid_spec=grid_spec,
        compiler_params=pltpu.CompilerParams(
            dimension_semantics=("arbitrary",), vmem_limit_bytes=VMEM_LIMIT),
        name="ffn",
    )(block_e, n_used, xg, wgu, bgu, wd, bd)


def _combine_kernel(n8_ref, lo_ref, gs_ref, yb_hbm, lpos_ref, gate_ref, h1_ref, g2_ref, b2_ref,
                    outa_ref, outb_ref, ys_ref, sem, *, tiles_a):
    tm = h1_ref.shape[0]
    nloc = ys_ref.shape[1]
    i = pl.program_id(0)
    slot = i % 2

    def copies(t, s):
        return _segment_copies(n8_ref, lo_ref, gs_ref, t,
                               lambda r, n: ys_ref.at[s, pl.ds(r, n), :],
                               lambda r, n: yb_hbm.at[pl.ds(r, n), :], sem.at[s], False)

    @pl.when(i == 0)
    def _():
        ys_ref[...] = jnp.zeros_like(ys_ref)
        copies(0, 0)(lambda d: d.start())

    @pl.when(i + 1 < pl.num_programs(0))
    def _():
        copies(i + 1, 1 - slot)(lambda d: d.start())

    copies(i, slot)(lambda d: d.wait())

    s_iota = lax.broadcasted_iota(jnp.int32, (tm, nloc), 1)
    w = jnp.zeros((tm, nloc), F32)
    for k in range(TOP_K):
        w = jnp.where(s_iota == lpos_ref[:, k:k + 1], gate_ref[:, k:k + 1], w)
    moe = jnp.dot(w.astype(BF16), ys_ref[slot].astype(BF16), preferred_element_type=F32)
    out = _ln(ALPHA * h1_ref[...] + moe, g2_ref[...], b2_ref[...])

    @pl.when(i < tiles_a)
    def _():
        outa_ref[...] = out

    @pl.when(i >= tiles_a)
    def _():
        outb_ref[...] = out


def _combine(n8f, lof, gsf, ybuf, lpos_t, gate_t, h1, g2, b2, tm, n_a):
    n = h1.shape[0]
    tiles_a = n_a // tm
    nloc = TOP_K * tm + N_EXPERTS * (SEG_ALIGN - 1)
    nloc = -(-nloc // SEG_ALIGN) * SEG_ALIGN
    grid_spec = pltpu.PrefetchScalarGridSpec(
        num_scalar_prefetch=3,
        grid=(n // tm,),
        in_specs=[
            pl.BlockSpec(memory_space=pl.ANY),
            pl.BlockSpec((tm, TOP_K), lambda i, *_: (i, 0)),
            pl.BlockSpec((tm, TOP_K), lambda i, *_: (i, 0)),
            pl.BlockSpec((tm, D_MODEL), lambda i, *_: (i, 0)),
            pl.BlockSpec((1, D_MODEL), lambda i, *_: (0, 0)),
            pl.BlockSpec((1, D_MODEL), lambda i, *_: (0, 0)),
        ],
        out_specs=(pl.BlockSpec((tm, D_MODEL), lambda i, *_: (jnp.minimum(i, tiles_a - 1), 0)),
                   pl.BlockSpec((tm, D_MODEL), lambda i, *_: (jnp.maximum(i - tiles_a, 0), 0))),
        scratch_shapes=[pltpu.VMEM((2, nloc, D_MODEL), F32), pltpu.SemaphoreType.DMA((2,))],
    )
    return pl.pallas_call(
        functools.partial(_combine_kernel, tiles_a=tiles_a),
        out_shape=(jax.ShapeDtypeStruct((n_a, D_MODEL), F32),
                   jax.ShapeDtypeStruct((n - n_a, D_MODEL), F32)),
        grid_spec=grid_spec,
        compiler_params=pltpu.CompilerParams(
            dimension_semantics=("arbitrary",), vmem_limit_bytes=VMEM_LIMIT),
        name="combine",
    )(n8f, lof, gsf, ybuf, lpos_t, gate_t, h1, g2, b2)


def _pick_tile(t, pref):
    return pref if t % pref == 0 else t


def kernel(x_prompt, x_sample, cache_meta_k, cache_meta_v, cache_win_k, cache_win_v, state_conv,
           meta_tokens, ln_in_g, ln_in_b, w_in, b_in, conv_w, attn_sinks, w_attn_br, w_conv_br, w_o,
           ln1_g, ln1_b, w_router, b_router, w_gu, b_gu, w_d, b_d, ln2_g, ln2_b):
    bp, tp, _ = x_prompt.shape
    bs, ts, _ = x_sample.shape
    row2 = lambda a: a.reshape(1, -1)

    w_in_b = w_in[0].astype(BF16)
    b_in_r = row2(b_in[0])
    gin, bin_ = row2(ln_in_g), row2(ln_in_b)
    wa, wc, wo = w_attn_br[0].astype(BF16), w_conv_br[0].astype(BF16), w_o[0].astype(BF16)
    wr_t = w_router[0].T.astype(BF16)
    br = b_router[0].reshape(N_EXPERTS, 1)
    wgu = _wprep(w_gu[0])
    bgu = (b_gu[0].reshape(N_EXPERTS, 2 * D_FF // GU_GROUP, LANES, 2).transpose(0, 1, 3, 2)
           .reshape(N_EXPERTS, 1, 2 * D_FF))
    wd = w_d[0].astype(BF16)
    bd = b_d[0][:, None, :]
    sinks = attn_sinks[0]

    zbuf = jnp.zeros((1, CONV_W - 1, D_CONV), F32)
    _, _, _, _, km, vm, _, _, um_last = _inproj(meta_tokens[None], zbuf, w_in_b, b_in_r, conv_w[0],
                                          gin, bin_, N_META)

    n_total = bp * tp + bs * ts

    def stream(x, cbuf, past_k, past_v, mk, mv, past_valid, tm_pref, row_off, prev):
        bsz, t, _ = x.shape
        q, ga, gc, oc, k, v, kb, vb, u_last = _inproj(x, cbuf, w_in_b, b_in_r, conv_w[0], gin, bin_,
                                             _pick_tile(t, tm_pref))
        tpad = -(-t // CHUNK) * CHUNK
        padq = ((0, 0), (0, tpad - t), (0, 0))
        kfull = jnp.concatenate([past_k.astype(BF16), jnp.pad(kb, padq)], axis=1)
        vfull = jnp.concatenate([past_v.astype(BF16), jnp.pad(vb, padq)], axis=1)
        oa = _attention(jnp.pad(q, padq), kfull, vfull, mk.astype(BF16), mv.astype(BF16), sinks,
                        past_valid, t, _pick_tile(tpad, tm_pref))[:, :t]
        n = bsz * t
        flat = lambda a: a.reshape(n, a.shape[-1])
        h1, lpos, gate, n8_tab = _mix(flat(x), flat(oa), flat(oc), flat(ga), flat(gc), wa, wc, wo,
                                  gin, bin_, row2(ln1_g[0]), row2(ln1_b[0]), wr_t, br,
                                  _pick_tile(n, tm_pref), n_total, row_off, prev)
        return h1, lpos, gate, n8_tab, k, v, u_last

    zk = jnp.zeros((bp, WINDOW, KV_W), F32)
    h1p, lposp, gatep, n8p, kp, vp, ulp = stream(x_prompt, um_last, zk, zk, km, vm, False, 512, 0, None)
    wk_s = cache_win_k[0].reshape(bs, WINDOW, KV_W)
    wv_s = cache_win_v[0].reshape(bs, WINDOW, KV_W)
    h1, lpos, gate, n8_tab, ks, vs, uls = stream(
        x_sample, state_conv[0], wk_s, wv_s, cache_meta_k[0].reshape(bs, N_META, KV_W),
        cache_meta_v[0].reshape(bs, N_META, KV_W), True, 512, bp * tp, (h1p, lposp, gatep, n8p))

    n = n_total
    tmm = TOKEN_TILE
    ntile = n // tmm
    t8 = n8_tab[:, :, 0].astype(jnp.int32)
    padded = (jnp.sum(t8, axis=0) + EXPERT_BLOCK - 1) // EXPERT_BLOCK * EXPERT_BLOCK
    pend = jnp.cumsum(padded)
    poff = pend - padded
    gstart = poff[None, :] + jnp.cumsum(t8, axis=0) - t8
    lstart = jnp.cumsum(t8, axis=1) - t8
    nb = -(-(n * TOP_K + ntile * N_EXPERTS * (SEG_ALIGN - 1)) // EXPERT_BLOCK) + N_EXPERTS
    block_e = jnp.minimum(
        jnp.sum((pend[None, :] <= (jnp.arange(nb) * EXPERT_BLOCK)[:, None]).astype(jnp.int32), axis=1),
        N_EXPERTS - 1).astype(jnp.int32)
    n_used = (pend[-1:] // EXPERT_BLOCK).astype(jnp.int32)
    last_blk = jnp.concatenate([jnp.where(padded > 0, pend - EXPERT_BLOCK, -1), pend[-1:]]).astype(jnp.int32)
    n8f, lof, gsf = t8.reshape(-1), lstart.reshape(-1).astype(jnp.int32), gstart.reshape(-1).astype(jnp.int32)

    xg = _dispatch(last_blk, n8f, lof, gsf, lpos, h1, nb * EXPERT_BLOCK, tmm)
    ybuf = _ffn(block_e, n_used, xg, wgu, bgu, wd, bd)
    y_p, y_s = _combine(n8f, lof, gsf, ybuf, lpos.T, gate.T, h1, row2(ln2_g[0]), row2(ln2_b[0]),
                        tmm, bp * tp)
    y_prompt = y_p.reshape(bp, tp, D_MODEL)
    y_sample = y_s.reshape(bs, ts, D_MODEL)
    kv5 = lambda a: a.reshape(a.shape[0], a.shape[1], N_KV, HEAD_DIM)[None]
    mk_p = jnp.broadcast_to(kv5(km), (1, bp, N_META, N_KV, HEAD_DIM))
    mv_p = jnp.broadcast_to(kv5(vm), (1, bp, N_META, N_KV, HEAD_DIM))
    wk_p = kv5(kp[:, -WINDOW:])
    wv_p = kv5(vp[:, -WINDOW:])
    wk_o = kv5(jnp.concatenate([wk_s, ks], axis=1)[:, -WINDOW:])
    wv_o = kv5(jnp.concatenate([wv_s, vs], axis=1)[:, -WINDOW:])
    return (y_prompt, y_sample, mk_p, mv_p, wk_p, wv_p, ulp[None], wk_o, wv_o, uls[None])
```
